```python
import jax
import jax.numpy as jnp
from jax import lax
import numpy as np

D_MODEL = 4096
BATCH = 4
SEQ = 2048
DEPTH = 4
DEC_BATCH = 128
DEC_SEQ = 4
PAST_LEN = 8192
PAGE_SIZE = 128

N_MEM = 256
N_BRANCH = 4
BR_W = D_MODEL // 4
EPS = 1e-6
NEG = -1e30
Q_BLOCK = 128
H_A = 8
NOPE_A = 128
ROPE_A = 64
V_A = 128
Q_RANK = 768
KV_RANK = 224
THETA_A = 10000.0
MLA_SCALE = (NOPE_A + ROPE_A) ** -0.5
H_B = 16
HD_B = 64
ROT_B = HD_B // 4
THETA_P = 500000.0
MOBA_BLOCK = 256
MOBA_TOPK = 3
MOBA_QCHUNK = 16
MOBA_SCALE = HD_B ** -0.5
H_C = 16
HD_C = 64
FOX_SCALE = HD_C ** -0.5
FOX_BIAS_MEAN = 2.0
H_M = 4
HD_M = 256
MEM_SCALE = HD_M ** -0.5

IN_SPLITS = (
    ('a_cq', Q_RANK), ('a_ckv', KV_RANK), ('a_kr', ROPE_A), ('a_gate', BR_W),
    ('b_q', H_B * HD_B), ('b_k', HD_B), ('b_v', HD_B), ('b_gate', BR_W),
    ('c_q', H_C * HD_C), ('c_k', HD_C), ('c_v', HD_C), ('c_f', H_C), ('c_gate', BR_W),
    ('m_q', H_M * HD_M), ('m_gate', BR_W),
    ('merge', N_BRANCH * D_MODEL),
)
N_IN = sum(size for _, size in IN_SPLITS)
STATE_ROWS = ('ckv', 'kr', 'kb', 'vb', 'kc', 'vc', 'logf')

kernel_name = 'hybrid_mla_moba_fox_mem_step'


def rmsnorm(x, g):
    xf = x.astype(jnp.float32)
    xf = xf * lax.rsqrt(jnp.mean(xf * xf, axis=-1, keepdims=True) + EPS)
    return (xf * g.astype(jnp.float32)).astype(x.dtype)


def rope(x, pos, theta):
    half = x.shape[-1] // 2
    inv = theta ** (-jnp.arange(half, dtype=jnp.float32) / half)
    ang = pos.astype(jnp.float32)[:, None] * inv[None, :]
    cos, sin = jnp.cos(ang)[:, None, :], jnp.sin(ang)[:, None, :]
    xf = x.astype(jnp.float32)
    x1, x2 = xf[..., :half], xf[..., half:]
    return jnp.concatenate([x1 * cos - x2 * sin, x2 * cos + x1 * sin], axis=-1).astype(x.dtype)


def partial_rope(x, pos):
    return jnp.concatenate([rope(x[..., :ROT_B], pos, THETA_P), x[..., ROT_B:]], axis=-1)


def split_cols(p):
    parts, off = {}, 0
    for name, size in IN_SPLITS:
        parts[name] = p[..., off:off + size]
        off += size
    return parts


def causal_mask(qpos, kpos):
    return kpos[None, :] <= qpos[:, None]


def in_proj(x, pos, g_norm, w_in, g_q_lora, w_uq, g_kv_lora, g_mla_qn, g_mla_qr, g_mla_kr,
            g_moba_q, g_moba_k, g_fox_q, g_fox_k, b_fox_f, g_mem_q):
    B, T, _ = x.shape
    c = split_cols(rmsnorm(x, g_norm) @ w_in)
    q_a = (rmsnorm(c['a_cq'], g_q_lora) @ w_uq).reshape(B, T, H_A, NOPE_A + ROPE_A)
    kr = rmsnorm(c['a_kr'], g_mla_kr)[:, :, None, :]
    kb = rmsnorm(c['b_k'], g_moba_k)[:, :, None, :]
    f_logit = c['c_f'].astype(jnp.float32) + b_fox_f.astype(jnp.float32)
    return {
        'qn': rmsnorm(q_a[..., :NOPE_A], g_mla_qn),
        'qr': rope(rmsnorm(q_a[..., NOPE_A:], g_mla_qr), pos, THETA_A),
        'ckv': rmsnorm(c['a_ckv'], g_kv_lora),
        'kr': rope(kr, pos, THETA_A)[:, :, 0, :],
        'qb': partial_rope(rmsnorm(c['b_q'].reshape(B, T, H_B, HD_B), g_moba_q), pos),
        'kb': partial_rope(kb, pos)[:, :, 0, :],
        'vb': c['b_v'],
        'qc': rmsnorm(c['c_q'].reshape(B, T, H_C, HD_C), g_fox_q),
        'kc': rmsnorm(c['c_k'], g_fox_k),
        'vc': c['c_v'],
        'logf': jax.nn.log_sigmoid(f_logit).astype(x.dtype),
        'qm': rmsnorm(c['m_q'].reshape(B, T, H_M, HD_M), g_mem_q),
        'gates': jnp.stack([c['a_gate'], c['b_gate'], c['c_gate'], c['m_gate']], axis=2),
        'merge': c['merge'].reshape(B, T, N_BRANCH, D_MODEL),
    }


def mla_core(qn, qr, qpos, ckv, kn, kr, kpos, w_uv):
    logits = (jnp.einsum('bqhd,bkhd->bhqk', qn, kn, preferred_element_type=jnp.float32)
              + jnp.einsum('bqhr,bkr->bhqk', qr, kr, preferred_element_type=jnp.float32)) * MLA_SCALE
    logits = jnp.where(causal_mask(qpos, kpos), logits, NEG)
    p = jax.nn.softmax(logits, axis=-1).astype(ckv.dtype)
    lat = jnp.einsum('bhqk,bkr->bqhr', p, ckv)
    return jnp.einsum('bqhr,rhd->bqhd', lat, w_uv.reshape(KV_RANK, H_A, V_A))


def fox_core(q, qpos, cq, k, v, kpos, ck):
    logits = jnp.einsum('bqhd,bkd->bhqk', q, k, preferred_element_type=jnp.float32) * FOX_SCALE
    decay = jnp.swapaxes(cq, 1, 2)[..., :, None] - jnp.swapaxes(ck, 1, 2)[..., None, :]
    logits = jnp.where(causal_mask(qpos, kpos), logits + decay, NEG)
    p = jax.nn.softmax(logits, axis=-1).astype(v.dtype)
    return jnp.einsum('bhqk,bkd->bqhd', p, v)


def moba_blocks(k, v):
    B, L, d = k.shape
    nb = -(-L // MOBA_BLOCK)
    pad = ((0, 0), (0, nb * MOBA_BLOCK - L), (0, 0))
    kblk = jnp.pad(k, pad).reshape(B, nb, MOBA_BLOCK, d)
    vblk = jnp.pad(v, pad).reshape(B, nb, MOBA_BLOCK, d)
    kmean = jnp.mean(kblk.astype(jnp.float32), axis=2).astype(k.dtype)
    return kblk, vblk, kmean


def moba_core(q, qpos, kblk, vblk, kmean):
    B, Tq, H, _ = q.shape
    nb = kblk.shape[1]
    own = qpos // MOBA_BLOCK
    gate = jnp.einsum('bqhd,bnd->bqhn', q, kmean, preferred_element_type=jnp.float32)
    fully_past = jnp.arange(nb)[None, :] < own[:, None]
    gate = jnp.where(fully_past[None, :, None, :], gate, NEG)
    n_sel = min(MOBA_TOPK, nb)
    _, idx = lax.top_k(gate, n_sel)
    valid = idx < own[None, :, None, None]
    bidx = jnp.arange(B)[:, None, None, None]
    k_sel, v_sel = kblk[bidx, idx], vblk[bidx, idx]
    k_own, v_own = kblk[:, own], vblk[:, own]
    l_sel = jnp.einsum('bqhd,bqhjnd->bqhjn', q, k_sel, preferred_element_type=jnp.float32) * MOBA_SCALE
    l_sel = jnp.where(valid[..., None], l_sel, NEG).reshape(B, Tq, H, n_sel * MOBA_BLOCK)
    l_own = jnp.einsum('bqhd,bqnd->bqhn', q, k_own, preferred_element_type=jnp.float32) * MOBA_SCALE
    own_pos = own[:, None] * MOBA_BLOCK + jnp.arange(MOBA_BLOCK)[None, :]
    l_own = jnp.where((own_pos <= qpos[:, None])[None, :, None, :], l_own, NEG)
    p = jax.nn.softmax(jnp.concatenate([l_sel, l_own], axis=-1), axis=-1).astype(vblk.dtype)
    p_sel = p[..., :n_sel * MOBA_BLOCK].reshape(B, Tq, H, n_sel, MOBA_BLOCK)
    p_own = p[..., n_sel * MOBA_BLOCK:]
    return (jnp.einsum('bqhjn,bqhjnd->bqhd', p_sel, v_sel)
            + jnp.einsum('bqhn,bqnd->bqhd', p_own, v_own))


def mem_kv(mem, g_mem_norm, w_mem_kv, g_mem_k):
    B, M, _ = mem.shape
    kv = (rmsnorm(mem, g_mem_norm) @ w_mem_kv).reshape(B, M, 2, H_M, HD_M)
    return rmsnorm(kv[:, :, 0], g_mem_k), kv[:, :, 1]


def mem_core(q, mk, mv):
    B, T = q.shape[:2]
    logits = jnp.einsum('bqhd,bmhd->bhqm', q, mk, preferred_element_type=jnp.float32) * MEM_SCALE
    p = jax.nn.softmax(logits, axis=-1).astype(mv.dtype)
    return jnp.einsum('bhqm,bmhd->bqhd', p, mv).reshape(B, T, BR_W)


def out_proj(x, outs, gates, merge, b_merge, w_branch, w_out):
    o = jnp.stack(outs, axis=2) * jax.nn.silu(gates)
    y = jnp.einsum('btnc,ncd->btnd', o, w_branch)
    return x + jnp.sum(jax.nn.sigmoid(merge + b_merge) * y, axis=2) @ w_out


def unblock(o):
    n, B, blk = o.shape[:3]
    return jnp.moveaxis(o, 0, 1).reshape(B, n * blk, -1)


def prompt_mixers(pr, w_uk, g_mla_kn, w_uv):
    B, T = pr['qn'].shape[:2]
    kpos = jnp.arange(T)
    kn = rmsnorm((pr['ckv'] @ w_uk).reshape(B, T, H_A, NOPE_A), g_mla_kn)
    c_cum = jnp.cumsum(pr['logf'].astype(jnp.float32), axis=1)

    def dense_block(i):
        s = i * Q_BLOCK
        qpos = s + jnp.arange(Q_BLOCK)
        sl = lambda a: lax.dynamic_slice_in_dim(a, s, Q_BLOCK, axis=1)
        o_a = mla_core(sl(pr['qn']), sl(pr['qr']), qpos, pr['ckv'], kn, pr['kr'], kpos, w_uv)
        o_c = fox_core(sl(pr['qc']), qpos, sl(c_cum), pr['kc'], pr['vc'], kpos, c_cum)
        return o_a, o_c

    o_a, o_c = lax.map(dense_block, jnp.arange(T // Q_BLOCK))
    kblk, vblk, kmean = moba_blocks(pr['kb'], pr['vb'])

    def moba_chunk(i):
        s = i * MOBA_QCHUNK
        q = lax.dynamic_slice_in_dim(pr['qb'], s, MOBA_QCHUNK, axis=1)
        return moba_core(q, s + jnp.arange(MOBA_QCHUNK), kblk, vblk, kmean)

    o_b = lax.map(moba_chunk, jnp.arange(T // MOBA_QCHUNK))
    return unblock(o_a), unblock(o_b), unblock(o_c)


def sample_mixers(sr, page_table, l, cache_mla_ckv, cache_mla_krope, cache_moba_k, cache_moba_v,
                  cache_fox_k, cache_fox_v, cache_fox_logf, w_uk, g_mla_kn, w_uv):
    past_len = page_table.shape[1] * PAGE_SIZE
    n_new = sr['qn'].shape[1]
    L = past_len + n_new
    qpos = past_len + jnp.arange(n_new)
    kpos = jnp.arange(L)

    def one_seq(args):
        pages, qn, qr, ckv_n, kr_n, qb, kb_n, vb_n, qc, kc_n, vc_n, lf_n = args

        def rows(pool, new):
            past = pool[l, pages].reshape(past_len, pool.shape[-1])
            return jnp.concatenate([past, new], axis=0)[None]

        ckv = rows(cache_mla_ckv, ckv_n)
        kn = rmsnorm((ckv @ w_uk).reshape(1, L, H_A, NOPE_A), g_mla_kn)
        o_a = mla_core(qn[None], qr[None], qpos, ckv, kn, rows(cache_mla_krope, kr_n), kpos, w_uv)
        kblk, vblk, kmean = moba_blocks(rows(cache_moba_k, kb_n), rows(cache_moba_v, vb_n))
        o_b = moba_core(qb[None], qpos, kblk, vblk, kmean)
        c_cum = jnp.cumsum(rows(cache_fox_logf, lf_n).astype(jnp.float32), axis=1)
        o_c = fox_core(qc[None], qpos, c_cum[:, past_len:], rows(cache_fox_k, kc_n),
                       rows(cache_fox_v, vc_n), kpos, c_cum)
        return (o_a[0].reshape(n_new, BR_W), o_b[0].reshape(n_new, BR_W),
                o_c[0].reshape(n_new, BR_W))

    return lax.map(one_seq, (page_table, sr['qn'], sr['qr'], sr['ckv'], sr['kr'], sr['qb'],
                             sr['kb'], sr['vb'], sr['qc'], sr['kc'], sr['vc'], sr['logf']))


def setup_inputs(seed: int = 0) -> dict:
    key = jax.random.key(seed)
    ks = iter(jax.random.split(key, 48))
    f32 = jnp.float32

    def nrm(shape, scale):
        return scale * jax.random.normal(next(ks), shape, f32)

    def unit(shape):
        return jax.random.normal(next(ks), shape, f32)

    def gain(shape):
        return 1.0 + 0.05 * jax.random.normal(next(ks), shape, f32)

    n_pages = PAST_LEN // PAGE_SIZE
    n_used = DEC_BATCH * n_pages
    n_pool = n_used + n_used // 4
    page_table = jax.random.permutation(next(ks), n_pool)[:n_used].reshape(DEC_BATCH, n_pages).astype(jnp.int32)

    def pool(w):
        return unit((DEPTH, n_pool, PAGE_SIZE, w))

    return {
        'x_prompt': unit((BATCH, SEQ, D_MODEL)),
        'x_sample': unit((DEC_BATCH, DEC_SEQ, D_MODEL)),
        'mem_prompt': unit((BATCH, N_MEM, D_MODEL)),
        'cache_mla_ckv': pool(KV_RANK),
        'cache_mla_krope': pool(ROPE_A),
        'cache_moba_k': pool(HD_B),
        'cache_moba_v': pool(HD_B),
        'cache_fox_k': pool(HD_C),
        'cache_fox_v': pool(HD_C),
        'cache_fox_logf': jax.nn.log_sigmoid(FOX_BIAS_MEAN + nrm((DEPTH, n_pool, PAGE_SIZE, H_C), 0.5)),
        'cache_mem_k': unit((DEPTH, DEC_BATCH, N_MEM, H_M, HD_M)),
        'cache_mem_v': unit((DEPTH, DEC_BATCH, N_MEM, H_M, HD_M)),
        'page_table': page_table,
        'g_norm': gain((DEPTH, D_MODEL)),
        'w_in': nrm((DEPTH, D_MODEL, N_IN), D_MODEL ** -0.5),
        'g_q_lora': gain((DEPTH, Q_RANK)),
        'w_uq': nrm((DEPTH, Q_RANK, H_A * (NOPE_A + ROPE_A)), Q_RANK ** -0.5),
        'g_kv_lora': gain((DEPTH, KV_RANK)),
        'w_uk': nrm((DEPTH, KV_RANK, H_A * NOPE_A), KV_RANK ** -0.5),
        'w_uv': nrm((DEPTH, KV_RANK, H_A * V_A), KV_RANK ** -0.5),
        'g_mla_qn': gain((DEPTH, NOPE_A)),
        'g_mla_kn': gain((DEPTH, NOPE_A)),
        'g_mla_qr': gain((DEPTH, ROPE_A)),
        'g_mla_kr': gain((DEPTH, ROPE_A)),
        'g_moba_q': gain((DEPTH, HD_B)),
        'g_moba_k': gain((DEPTH, HD_B)),
        'g_fox_q': gain((DEPTH, HD_C)),
        'g_fox_k': gain((DEPTH, HD_C)),
        'b_fox_f': FOX_BIAS_MEAN + nrm((DEPTH, H_C), 0.5),
        'g_mem_norm': gain((DEPTH, D_MODEL)),
        'w_mem_kv': nrm((DEPTH, D_MODEL, 2 * H_M * HD_M), D_MODEL ** -0.5),
        'g_mem_q': gain((DEPTH, HD_M)),
        'g_mem_k': gain((DEPTH, HD_M)),
        'b_merge': nrm((DEPTH, N_BRANCH, D_MODEL), 0.1),
        'w_branch': nrm((DEPTH, N_BRANCH, BR_W, D_MODEL), BR_W ** -0.5),
        'w_out': nrm((DEPTH, D_MODEL, D_MODEL), D_MODEL ** -0.5),
    }


def reference(x_prompt, x_sample, mem_prompt, cache_mla_ckv, cache_mla_krope, cache_moba_k,
              cache_moba_v, cache_fox_k, cache_fox_v, cache_fox_logf, cache_mem_k, cache_mem_v,
              page_table, g_norm, w_in, g_q_lora, w_uq, g_kv_lora, w_uk, w_uv, g_mla_qn, g_mla_kn,
              g_mla_qr, g_mla_kr, g_moba_q, g_moba_k, g_fox_q, g_fox_k, b_fox_f, g_mem_norm,
              w_mem_kv, g_mem_q, g_mem_k, b_merge, w_branch, w_out):
    pos_p = jnp.arange(x_prompt.shape[1])
    pos_s = page_table.shape[1] * PAGE_SIZE + jnp.arange(x_sample.shape[1])
    xp, xs = x_prompt, x_sample
    sp = {k: [] for k in STATE_ROWS + ('mk', 'mv')}
    ss = {k: [] for k in STATE_ROWS}
    for l in range(DEPTH):
        def proj(x, pos):
            return in_proj(x, pos, g_norm[l], w_in[l], g_q_lora[l], w_uq[l], g_kv_lora[l],
                           g_mla_qn[l], g_mla_qr[l], g_mla_kr[l], g_moba_q[l], g_moba_k[l],
                           g_fox_q[l], g_fox_k[l], b_fox_f[l], g_mem_q[l])

        pr = proj(xp, pos_p)
        mk, mv = mem_kv(mem_prompt, g_mem_norm[l], w_mem_kv[l], g_mem_k[l])
        o_a, o_b, o_c = prompt_mixers(pr, w_uk[l], g_mla_kn[l], w_uv[l])
        o_m = mem_core(pr['qm'], mk, mv)
        xp = out_proj(xp, (o_a, o_b, o_c, o_m), pr['gates'], pr['merge'], b_merge[l],
                      w_branch[l], w_out[l])
        sr = proj(xs, pos_s)
        s_a, s_b, s_c = sample_mixers(sr, page_table, l, cache_mla_ckv, cache_mla_krope,
                                      cache_moba_k, cache_moba_v, cache_fox_k, cache_fox_v,
                                      cache_fox_logf, w_uk[l], g_mla_kn[l], w_uv[l])
        s_m = mem_core(sr['qm'], cache_mem_k[l], cache_mem_v[l])
        xs = out_proj(xs, (s_a, s_b, s_c, s_m), sr['gates'], sr['merge'], b_merge[l],
                      w_branch[l], w_out[l])
        for k in STATE_ROWS:
            sp[k].append(pr[k])
            ss[k].append(sr[k])
        sp['mk'].append(mk)
        sp['mv'].append(mv)
    return (xp, xs,
            jnp.stack(sp['ckv']), jnp.stack(sp['kr']), jnp.stack(sp['kb']), jnp.stack(sp['vb']),
            jnp.stack(sp['kc']), jnp.stack(sp['vc']), jnp.stack(sp['logf']),
            jnp.stack(sp['mk']), jnp.stack(sp['mv']),
            jnp.stack(ss['ckv']), jnp.stack(ss['kr']), jnp.stack(ss['kb']), jnp.stack(ss['vb']),
            jnp.stack(ss['kc']), jnp.stack(ss['vc']), jnp.stack(ss['logf']))
```

```python
import functools

import jax
import jax.numpy as jnp
from jax import lax
from jax.experimental import pallas as pl
from jax.experimental.pallas import tpu as pltpu

F32 = jnp.float32
BF16 = jnp.bfloat16

D_MODEL = 4096
PAGE_SIZE = 128
N_BRANCH = 4
BR_W = D_MODEL // 4
EPS = 1e-6
NEG = -1e30
H_A, NOPE_A, ROPE_A, V_A, Q_RANK, KV_RANK = 8, 128, 64, 128, 768, 224
THETA_A = 10000.0
MLA_SCALE = (NOPE_A + ROPE_A) ** -0.5
H_B, HD_B = 16, 64
ROT_B = HD_B // 4
THETA_P = 500000.0
MOBA_BLOCK = 256
MOBA_TOPK = 3
MOBA_SCALE = HD_B ** -0.5
H_C, HD_C = 16, 64
FOX_SCALE = HD_C ** -0.5
H_M, HD_M = 4, 256
MEM_SCALE = HD_M ** -0.5

IN_SPLITS = (
    ('a_cq', Q_RANK), ('a_ckv', KV_RANK), ('a_kr', ROPE_A), ('a_gate', BR_W),
    ('b_q', H_B * HD_B), ('b_k', HD_B), ('b_v', HD_B), ('b_gate', BR_W),
    ('c_q', H_C * HD_C), ('c_k', HD_C), ('c_v', HD_C), ('c_f', H_C), ('c_gate', BR_W),
    ('m_q', H_M * HD_M), ('m_gate', BR_W),
    ('merge', N_BRANCH * D_MODEL),
)
IN_OFF = {}
_off = 0
for _name, _size in IN_SPLITS:
    IN_OFF[_name] = (_off, _size)
    _off += _size

SMALL_COLS = ('a_cq', 'a_ckv', 'a_kr', 'b_k', 'b_v', 'c_k', 'c_v', 'c_f')
SMALL_W = 1536
QG_COLS = ('a_gate', 'b_q', 'b_gate', 'c_q', 'c_gate', 'm_q', 'm_gate')
GATE_BLOCK = (0, 2, 4, 6)

VMEM_LIMIT = 56 * 1024 * 1024


def _tile(n, cap, mult=8):
    best = None
    for t in range(mult, min(n, cap) + 1, mult):
        if n % t == 0:
            best = t
    return best if best is not None else n


def _cparams(sem):
    return pltpu.CompilerParams(dimension_semantics=sem, vmem_limit_bytes=VMEM_LIMIT)


def _dot_nt(a, b):
    return lax.dot_general(a, b, (((1,), (1,)), ((), ())), preferred_element_type=F32)


def _split3(x):
    a = x.astype(BF16)
    r = x - a.astype(F32)
    b = r.astype(BF16)
    c = (r - b.astype(F32)).astype(BF16)
    return a, b, c


def _rmsnorm_kernel(x_ref, g_ref, o_ref):
    x = x_ref[...].astype(F32)
    r = lax.rsqrt(jnp.mean(x * x, axis=-1, keepdims=True) + EPS)
    o_ref[...] = ((x * r) * g_ref[...]).astype(o_ref.dtype)


def pl_rmsnorm(x, g, out_dtype=F32):
    m, d = x.shape
    tm = _tile(m, max(8, (2 * 1024 * 1024) // (4 * max(d, 128))), 8)
    return pl.pallas_call(
        _rmsnorm_kernel,
        out_shape=jax.ShapeDtypeStruct((m, d), out_dtype),
        grid=(m // tm,),
        in_specs=[pl.BlockSpec((tm, d), lambda i: (i, 0)),
                  pl.BlockSpec((1, d), lambda i: (0, 0))],
        out_specs=pl.BlockSpec((tm, d), lambda i: (i, 0)),
        compiler_params=_cparams(("arbitrary",)),
        name="rmsnorm",
    )(x, g.reshape(1, d).astype(F32))


def _logsig_kernel(x_ref, b_ref, o_ref):
    x = x_ref[...] + b_ref[...]
    o_ref[...] = jnp.minimum(x, 0.0) - jnp.log(1.0 + jnp.exp(-jnp.abs(x)))


def pl_log_sigmoid_bias(x, b):
    m, d = x.shape
    tm = _tile(m, 2048, 8)
    return pl.pallas_call(
        _logsig_kernel,
        out_shape=jax.ShapeDtypeStruct((m, d), F32),
        grid=(m // tm,),
        in_specs=[pl.BlockSpec((tm, d), lambda i: (i, 0)),
                  pl.BlockSpec((1, d), lambda i: (0, 0))],
        out_specs=pl.BlockSpec((tm, d), lambda i: (i, 0)),
        compiler_params=_cparams(("arbitrary",)),
        name="log_sigmoid_bias",
    )(x, b.reshape(1, d).astype(F32))


def _cumsum_kernel(x_ref, o_ref, carry_ref, *, blk):
    @pl.when(pl.program_id(1) == 0)
    def _():
        carry_ref[...] = jnp.zeros_like(carry_ref)

    row = lax.broadcasted_iota(jnp.int32, (blk, blk), 0)
    col = lax.broadcasted_iota(jnp.int32, (blk, blk), 1)
    tri = (col <= row).astype(BF16)
    x = x_ref[0]
    c = None
    for part in reversed(_split3(x)):
        t = jnp.dot(tri, part, preferred_element_type=F32)
        c = t if c is None else c + t
    c = c + carry_ref[...]
    o_ref[0] = c
    carry_ref[...] = c[blk - 1:blk, :]


def pl_cumsum_rows(x):
    b, t, w = x.shape
    blk = _tile(t, 256, 8)
    return pl.pallas_call(
        functools.partial(_cumsum_kernel, blk=blk),
        out_shape=jax.ShapeDtypeStruct((b, t, w), F32),
        grid=(b, t // blk),
        in_specs=[pl.BlockSpec((1, blk, w), lambda i, j: (i, j, 0))],
        out_specs=pl.BlockSpec((1, blk, w), lambda i, j: (i, j, 0)),
        scratch_shapes=[pltpu.VMEM((1, w), F32)],
        compiler_params=_cparams(("arbitrary", "arbitrary")),
        name="cumsum_rows",
    )(x)


def _mm_kernel(a_ref, w_ref, *rest, has_res):
    if has_res:
        r_ref, o_ref = rest
    else:
        (o_ref,) = rest
    acc = jnp.dot(a_ref[...], w_ref[...], preferred_element_type=F32)
    if has_res:
        acc = acc + r_ref[...]
    o_ref[...] = acc.astype(o_ref.dtype)


def pl_matmul(a, w, res=None, out_dtype=F32, tm_cap=1088, tn_cap=1024):
    m, k = a.shape
    _, n = w.shape
    tm = _tile(m, tm_cap, 16)
    tn = _tile(n, tn_cap, 128)
    in_specs = [pl.BlockSpec((tm, k), lambda i, j: (i, 0)),
                pl.BlockSpec((k, tn), lambda i, j: (0, j))]
    args = [a, w]
    if res is not None:
        in_specs.append(pl.BlockSpec((tm, tn), lambda i, j: (i, j)))
        args.append(res)
    return pl.pallas_call(
        functools.partial(_mm_kernel, has_res=res is not None),
        out_shape=jax.ShapeDtypeStruct((m, n), out_dtype),
        grid=(m // tm, n // tn),
        in_specs=in_specs,
        out_specs=pl.BlockSpec((tm, tn), lambda i, j: (i, j)),
        compiler_params=_cparams(("arbitrary", "arbitrary")),
        name="matmul",
    )(*args)


def _headproj_kernel(a_ref, w_ref, o_ref):
    o_ref[0] = jnp.dot(a_ref[0, 0].astype(BF16), w_ref[0], preferred_element_type=F32)


def pl_headproj(a, w):
    b, h, t, r = a.shape
    d = w.shape[2]
    tm = _tile(t, 1024, 8)
    return pl.pallas_call(
        _headproj_kernel,
        out_shape=jax.ShapeDtypeStruct((b, t, h * d), F32),
        grid=(b, h, t // tm),
        in_specs=[pl.BlockSpec((1, 1, tm, r), lambda bi, hi, i: (bi, hi, i, 0)),
                  pl.BlockSpec((1, r, d), lambda bi, hi, i: (hi, 0, 0))],
        out_specs=pl.BlockSpec((1, tm, d), lambda bi, hi, i: (bi, i, hi)),
        compiler_params=_cparams(("arbitrary", "arbitrary", "arbitrary")),
        name="headproj",
    )(a, w)


def _online_update(s, v_b, m_sc, l_sc, acc_sc):
    m_prev = m_sc[...]
    m_new = jnp.maximum(m_prev, jnp.max(s, axis=-1, keepdims=True))
    alpha = jnp.exp(m_prev - m_new)
    p = jnp.exp(s - m_new)
    l_sc[...] = alpha * l_sc[...] + jnp.sum(p, axis=-1, keepdims=True)
    acc_sc[...] = alpha * acc_sc[...] + jnp.dot(p.astype(BF16), v_b, preferred_element_type=F32)
    m_sc[...] = m_new


def _flash_kernel(*refs, scale, causal, tq, tk, nk, two, bias, bmask):
    it = iter(refs)
    q1_ref, k1_ref = next(it), next(it)
    q2_ref = k2_ref = rb_ref = cb_ref = mk_ref = None
    if two:
        q2_ref, k2_ref = next(it), next(it)
    v_ref = next(it)
    if bias:
        rb_ref, cb_ref = next(it), next(it)
    if bmask:
        mk_ref = next(it)
    o_ref, m_sc, l_sc, acc_sc = next(it), next(it), next(it), next(it)
    i = pl.program_id(2)
    j = pl.program_id(3)

    @pl.when(j == 0)
    def _():
        m_sc[...] = jnp.full_like(m_sc, NEG)
        l_sc[...] = jnp.zeros_like(l_sc)
        acc_sc[...] = jnp.zeros_like(acc_sc)

    def step():
        s = _dot_nt(q1_ref[0, 0].astype(BF16), k1_ref[0].astype(BF16))
        if two:
            s = s + _dot_nt(q2_ref[0, 0].astype(BF16), k2_ref[0].astype(BF16))
        s = s * scale
        if bias:
            s = s + rb_ref[0, 0] + cb_ref[0, 0]
        if causal:
            qpos = i * tq + lax.broadcasted_iota(jnp.int32, (tq, tk), 0)
            kpos = j * tk + lax.broadcasted_iota(jnp.int32, (tq, tk), 1)
            ok = kpos <= qpos
            if bmask:
                mk = mk_ref[0, 0]
                lane = lax.broadcasted_iota(jnp.int32, mk.shape, 1)
                sel = jnp.sum(jnp.where(lane == j, mk, 0.0), axis=-1, keepdims=True)
                sel = sel + jnp.where(j == i, 1.0, 0.0)
                ok = ok & (sel > 0.5)
            s = jnp.where(ok, s, NEG)
        _online_update(s, v_ref[0].astype(BF16), m_sc, l_sc, acc_sc)

    if causal:
        pl.when(j <= i)(step)
    else:
        step()

    @pl.when(j == nk - 1)
    def _():
        o_ref[0, 0] = acc_sc[...] / l_sc[...]


def pl_flash(q1, k1, v, *, scale, causal, q2=None, k2=None, rowb=None, colb=None, bmask=None, d_v=None, blk=256):
    b, h, tq_all, d1 = q1.shape
    tk_all = k1.shape[1]
    tq = _tile(tq_all, blk, 8)
    tk = _tile(tk_all, blk, 8)
    nq, nk = tq_all // tq, tk_all // tk
    if causal:
        assert tq == tk and tq_all == tk_all
    if bmask is not None:
        assert causal and tk == MOBA_BLOCK

    def kmap(arr, d):
        hk = arr.shape[2] // d
        if causal:
            return (lambda bi, hi, i, j: (bi, jnp.minimum(j, i), hi if hk > 1 else 0))
        return (lambda bi, hi, i, j: (bi, j, hi if hk > 1 else 0))

    qmap = lambda bi, hi, i, j: (bi, hi, i, 0)
    in_specs = [pl.BlockSpec((1, 1, tq, d1), qmap), pl.BlockSpec((1, tk, d1), kmap(k1, d1))]
    args = [q1, k1]
    if q2 is not None:
        d2 = q2.shape[3]
        in_specs += [pl.BlockSpec((1, 1, tq, d2), qmap), pl.BlockSpec((1, tk, d2), kmap(k2, d2))]
        args += [q2, k2]
    dv = v.shape[2] if d_v is None else d_v
    in_specs.append(pl.BlockSpec((1, tk, dv), kmap(v, dv)))
    args.append(v)
    if rowb is not None:
        in_specs += [pl.BlockSpec((1, 1, tq, 1), qmap),
                     pl.BlockSpec((1, 1, 1, tk), lambda bi, hi, i, j: (bi, hi, 0, jnp.minimum(j, i)))]
        args += [rowb, colb]
    if bmask is not None:
        nb = bmask.shape[3]
        in_specs.append(pl.BlockSpec((1, 1, tq, nb), qmap))
        args.append(bmask)
    kern = functools.partial(_flash_kernel, scale=scale, causal=causal, tq=tq, tk=tk, nk=nk,
                             two=q2 is not None, bias=rowb is not None, bmask=bmask is not None)
    return pl.pallas_call(
        kern,
        out_shape=jax.ShapeDtypeStruct((b, h, tq_all, dv), F32),
        grid=(b, h, nq, nk),
        in_specs=in_specs,
        out_specs=pl.BlockSpec((1, 1, tq, dv), qmap),
        scratch_shapes=[pltpu.VMEM((tq, 1), F32), pltpu.VMEM((tq, 1), F32), pltpu.VMEM((tq, dv), F32)],
        compiler_params=_cparams(("arbitrary",) * 4),
        name="flash_attention",
    )(*args)


def _topk_mask(g, valid, k):
    r, nb = g.shape
    g = jnp.where(valid, g, NEG)
    lane = lax.broadcasted_iota(jnp.int32, (r, nb), 1)
    rank = jnp.zeros((r, nb), jnp.int32)
    for m in range(nb):
        gm = g[:, m:m + 1]
        ahead = (gm > g) | ((gm == g) & (m < lane))
        rank = rank + ahead.astype(jnp.int32)
    return jnp.where((rank < k) & valid, 1.0, 0.0).astype(F32)


def _moba_gate_prompt_kernel(q_ref, k_ref, o_ref, *, nb, tq):
    i = pl.program_id(2)
    means = []
    for n in range(nb):
        kb = k_ref[0, n * MOBA_BLOCK:(n + 1) * MOBA_BLOCK, :]
        means.append(jnp.sum(kb, axis=0, keepdims=True) * (1.0 / MOBA_BLOCK))
    kmean = jnp.concatenate(means, axis=0)
    g = lax.dot_general(q_ref[0, 0], kmean, (((1,), (1,)), ((), ())),
                        precision=lax.Precision.HIGHEST, preferred_element_type=F32)
    own = (i * tq) // MOBA_BLOCK
    lane = lax.broadcasted_iota(jnp.int32, g.shape, 1)
    o_ref[0, 0] = _topk_mask(g, lane < own, MOBA_TOPK)


def pl_moba_gate_prompt(q, k):
    b, h, t, d = q.shape
    assert t % MOBA_BLOCK == 0
    nb = t // MOBA_BLOCK
    tq = MOBA_BLOCK
    return pl.pallas_call(
        functools.partial(_moba_gate_prompt_kernel, nb=nb, tq=tq),
        out_shape=jax.ShapeDtypeStruct((b, h, t, nb), F32),
        grid=(b, h, t // tq),
        in_specs=[pl.BlockSpec((1, 1, tq, d), lambda bi, hi, i: (bi, hi, i, 0)),
                  pl.BlockSpec((1, t, d), lambda bi, hi, i: (bi, 0, 0))],
        out_specs=pl.BlockSpec((1, 1, tq, nb), lambda bi, hi, i: (bi, hi, i, 0)),
        compiler_params=_cparams(("arbitrary",) * 3),
        name="moba_gate_prompt",
    )(q, k)


def _dec_init(m_sc, l_sc, acc_sc):
    m_sc[...] = jnp.full_like(m_sc, NEG)
    l_sc[...] = jnp.zeros_like(l_sc)
    acc_sc[...] = jnp.zeros_like(acc_sc)


def _new_token_mask(rows, heads):
    jrow = lax.broadcasted_iota(jnp.int32, (rows, PAGE_SIZE), 0) // heads
    lane = lax.broadcasted_iota(jnp.int32, (rows, PAGE_SIZE), 1)
    return lane <= jrow, lane, jrow


def _mla_dec_kernel(pt_ref, qbd_ref, qr_ref, ckv_ref, kr_ref, ckvn_ref, krn_ref, wuk_ref, g_ref,
                    o_ref, m_sc, l_sc, acc_sc, *, npg):
    p = pl.program_id(1)
    rows = qbd_ref.shape[1]

    @pl.when(p == 0)
    def _():
        _dec_init(m_sc, l_sc, acc_sc)

    def tile(ckv, kr, is_new):
        ckv_b = ckv.astype(BF16)
        kn = jnp.dot(ckv_b, wuk_ref[...], preferred_element_type=F32)
        parts = []
        for h in range(H_A):
            blk = kn[:, h * NOPE_A:(h + 1) * NOPE_A]
            r = lax.rsqrt(jnp.mean(blk * blk, axis=-1, keepdims=True) + EPS)
            parts.append(((blk * r) * g_ref[...]).astype(BF16))
        kn_n = jnp.concatenate(parts, axis=1)
        s = _dot_nt(qbd_ref[0].astype(BF16), kn_n) + _dot_nt(qr_ref[0].astype(BF16), kr.astype(BF16))
        s = s * MLA_SCALE
        if is_new:
            ok, _, _ = _new_token_mask(rows, H_A)
            s = jnp.where(ok, s, NEG)
        _online_update(s, ckv_b, m_sc, l_sc, acc_sc)

    @pl.when(p < npg)
    def _():
        tile(ckv_ref[0, 0], kr_ref[0, 0], False)

    @pl.when(p == npg)
    def _():
        tile(ckvn_ref[0], krn_ref[0], True)
        o_ref[0] = acc_sc[...] / l_sc[...]


def _paged_specs(layer, npg, widths, reverse=False):
    def pmap(s, p, pt):
        if reverse:
            idx = jnp.clip(npg - p, 0, npg - 1)
        else:
            idx = jnp.minimum(p, npg - 1)
        return (layer, pt[s, idx], 0, 0)
    return [pl.BlockSpec((1, 1, PAGE_SIZE, w), pmap) for w in widths]


def pl_mla_decode(layer, page_table, qbd, qr, cache_ckv, cache_kr, ckv_new, kr_new, w_uk, g_kn):
    s_, rows, _ = qbd.shape
    npg = page_table.shape[1]
    smap = lambda s, p, pt: (s, 0, 0)
    cmap = lambda s, p, pt: (0, 0)
    in_specs = ([pl.BlockSpec((1, rows, H_A * NOPE_A), smap), pl.BlockSpec((1, rows, ROPE_A), smap)]
                + _paged_specs(layer, npg, (KV_RANK, ROPE_A))
                + [pl.BlockSpec((1, PAGE_SIZE, KV_RANK), smap), pl.BlockSpec((1, PAGE_SIZE, ROPE_A), smap),
                   pl.BlockSpec((KV_RANK, H_A * NOPE_A), cmap), pl.BlockSpec((1, NOPE_A), cmap)])
    return pl.pallas_call(
        functools.partial(_mla_dec_kernel, npg=npg),
        out_shape=jax.ShapeDtypeStruct((s_, rows, KV_RANK), F32),
        grid_spec=pltpu.PrefetchScalarGridSpec(
            num_scalar_prefetch=1, grid=(s_, npg + 1), in_specs=in_specs,
            out_specs=pl.BlockSpec((1, rows, KV_RANK), smap),
            scratch_shapes=[pltpu.VMEM((rows, 1), F32), pltpu.VMEM((rows, 1), F32),
                            pltpu.VMEM((rows, KV_RANK), F32)]),
        compiler_params=_cparams(("arbitrary", "arbitrary")),
        name="mla_decode",
    )(page_table, qbd, qr, cache_ckv, cache_kr, ckv_new, kr_new, w_uk, g_kn.reshape(1, NOPE_A).astype(F32))


def _fox_dec_kernel(pt_ref, q_ref, k_ref, v_ref, lf_ref, kn_ref, vn_ref, lfn_ref,
                    o_ref, m_sc, l_sc, acc_sc, carry_sc, *, npg):
    p = pl.program_id(1)
    rows = q_ref.shape[1]
    r_i = lax.broadcasted_iota(jnp.int32, (rows, H_C), 0)
    c_i = lax.broadcasted_iota(jnp.int32, (rows, H_C), 1)
    sel = (r_i % H_C == c_i).astype(BF16)
    m_i = lax.broadcasted_iota(jnp.int32, (PAGE_SIZE, PAGE_SIZE), 0)
    k_i = lax.broadcasted_iota(jnp.int32, (PAGE_SIZE, PAGE_SIZE), 1)
    later = (m_i > k_i).astype(BF16)

    def tile(k, v, lf, is_new):
        x = None
        for part in reversed(_split3(lf)):
            t = _dot_nt(sel, part)
            x = t if x is None else x + t
        sfx = None
        for part in reversed(_split3(x)):
            t = jnp.dot(part, later, preferred_element_type=F32)
            sfx = t if sfx is None else sfx + t
        tot = jnp.sum(x, axis=-1, keepdims=True)
        s = _dot_nt(q_ref[0].astype(BF16), k.astype(BF16)) * FOX_SCALE
        if is_new:
            ok, lane, jrow = _new_token_mask(rows, H_C)
            sq = jnp.sum(jnp.where(lane == jrow, sfx, 0.0), axis=-1, keepdims=True)
            s = jnp.where(ok, s + (sfx - sq), NEG)
            carry_sc[...] = tot - sq
        else:
            s = s + (sfx + carry_sc[...])
            carry_sc[...] = carry_sc[...] + tot
        _online_update(s, v.astype(BF16), m_sc, l_sc, acc_sc)

    @pl.when(p == 0)
    def _():
        _dec_init(m_sc, l_sc, acc_sc)
        tile(kn_ref[0], vn_ref[0], lfn_ref[0], True)

    @pl.when(p > 0)
    def _():
        tile(k_ref[0, 0], v_ref[0, 0], lf_ref[0, 0], False)

    @pl.when(p == npg)
    def _():
        o_ref[0] = acc_sc[...] / l_sc[...]


def pl_fox_decode(layer, page_table, q, cache_k, cache_v, cache_lf, k_new, v_new, lf_new):
    s_, rows, d = q.shape
    npg = page_table.shape[1]
    smap = lambda s, p, pt: (s, 0, 0)
    in_specs = ([pl.BlockSpec((1, rows, d), smap)]
                + _paged_specs(layer, npg, (HD_C, HD_C, H_C), reverse=True)
                + [pl.BlockSpec((1, PAGE_SIZE, HD_C), smap), pl.BlockSpec((1, PAGE_SIZE, HD_C), smap),
                   pl.BlockSpec((1, PAGE_SIZE, H_C), smap)])
    return pl.pallas_call(
        functools.partial(_fox_dec_kernel, npg=npg),
        out_shape=jax.ShapeDtypeStruct((s_, rows, HD_C), F32),
        grid_spec=pltpu.PrefetchScalarGridSpec(
            num_scalar_prefetch=1, grid=(s_, npg + 1), in_specs=in_specs,
            out_specs=pl.BlockSpec((1, rows, HD_C), smap),
            scratch_shapes=[pltpu.VMEM((rows, 1), F32), pltpu.VMEM((rows, 1), F32),
                            pltpu.VMEM((rows, HD_C), F32), pltpu.VMEM((rows, 1), F32)]),
        compiler_params=_cparams(("arbitrary", "arbitrary")),
        name="fox_decode",
    )(page_table, q, cache_k, cache_v, cache_lf, k_new, v_new, lf_new)


def _moba_gate_dec_kernel(pt_ref, q_ref, k_ref, o_ref, km_sc, *, npg, ppb):
    p = pl.program_id(1)
    colsum = jnp.sum(k_ref[0, 0], axis=0, keepdims=True)
    blk = p // ppb

    @pl.when(p % ppb == 0)
    def _():
        km_sc[pl.ds(blk, 1), :] = colsum

    @pl.when(p % ppb != 0)
    def _():
        km_sc[pl.ds(blk, 1), :] = km_sc[pl.ds(blk, 1), :] + colsum

    @pl.when(p == npg - 1)
    def _():
        kmean = km_sc[...] * (1.0 / MOBA_BLOCK)
        g = lax.dot_general(q_ref[0], kmean, (((1,), (1,)), ((), ())),
                            precision=lax.Precision.HIGHEST, preferred_element_type=F32)
        o_ref[0] = _topk_mask(g, jnp.full(g.shape, True), MOBA_TOPK)


def pl_moba_gate_decode(layer, page_table, q, cache_k):
    s_, rows, d = q.shape
    npg = page_table.shape[1]
    ppb = MOBA_BLOCK // PAGE_SIZE
    assert npg % ppb == 0
    nb = npg // ppb
    smap = lambda s, p, pt: (s, 0, 0)
    return pl.pallas_call(
        functools.partial(_moba_gate_dec_kernel, npg=npg, ppb=ppb),
        out_shape=jax.ShapeDtypeStruct((s_, rows, nb), F32),
        grid_spec=pltpu.PrefetchScalarGridSpec(
            num_scalar_prefetch=1, grid=(s_, npg),
            in_specs=[pl.BlockSpec((1, rows, d), smap)] + _paged_specs(layer, npg, (HD_B,)),
            out_specs=pl.BlockSpec((1, rows, nb), smap),
            scratch_shapes=[pltpu.VMEM((nb, HD_B), F32)]),
        compiler_params=_cparams(("arbitrary", "arbitrary")),
        name="moba_gate_decode",
    )(page_table, q, cache_k)


def _moba_dec_kernel(pt_ref, q_ref, mk_ref, k_ref, v_ref, kn_ref, vn_ref,
                     o_ref, m_sc, l_sc, acc_sc, *, npg, ppb):
    p = pl.program_id(1)
    rows = q_ref.shape[1]

    @pl.when(p == 0)
    def _():
        _dec_init(m_sc, l_sc, acc_sc)

    def tile(k, v, is_new):
        s = _dot_nt(q_ref[0].astype(BF16), k.astype(BF16)) * MOBA_SCALE
        if is_new:
            ok, _, _ = _new_token_mask(rows, H_B)
        else:
            mk = mk_ref[0]
            lane = lax.broadcasted_iota(jnp.int32, mk.shape, 1)
            ok = jnp.sum(jnp.where(lane == p // ppb, mk, 0.0), axis=-1, keepdims=True) > 0.5
        s = jnp.where(ok, s, NEG)
        _online_update(s, v.astype(BF16), m_sc, l_sc, acc_sc)

    @pl.when(p < npg)
    def _():
        tile(k_ref[0, 0], v_ref[0, 0], False)

    @pl.when(p == npg)
    def _():
        tile(kn_ref[0], vn_ref[0], True)
        o_ref[0] = acc_sc[...] / l_sc[...]


def pl_moba_decode(layer, page_table, q, mask, cache_k, cache_v, k_new, v_new):
    s_, rows, d = q.shape
    npg = page_table.shape[1]
    ppb = MOBA_BLOCK // PAGE_SIZE
    nb = mask.shape[2]
    smap = lambda s, p, pt: (s, 0, 0)
    in_specs = ([pl.BlockSpec((1, rows, d), smap), pl.BlockSpec((1, rows, nb), smap)]
                + _paged_specs(layer, npg, (HD_B, HD_B))
                + [pl.BlockSpec((1, PAGE_SIZE, HD_B), smap), pl.BlockSpec((1, PAGE_SIZE, HD_B), smap)])
    return pl.pallas_call(
        functools.partial(_moba_dec_kernel, npg=npg, ppb=ppb),
        out_shape=jax.ShapeDtypeStruct((s_, rows, HD_B), F32),
        grid_spec=pltpu.PrefetchScalarGridSpec(
            num_scalar_prefetch=1, grid=(s_, npg + 1), in_specs=in_specs,
            out_specs=pl.BlockSpec((1, rows, HD_B), smap),
            scratch_shapes=[pltpu.VMEM((rows, 1), F32), pltpu.VMEM((rows, 1), F32),
                            pltpu.VMEM((rows, HD_B), F32)]),
        compiler_params=_cparams(("arbitrary", "arbitrary")),
        name="moba_decode",
    )(page_table, q, mask, cache_k, cache_v, k_new, v_new)


def _branch_merge_kernel(o_ref, g_ref, mg_ref, b_ref, w_ref, z_ref, acc_sc):
    n = pl.program_id(2)

    @pl.when(n == 0)
    def _():
        acc_sc[...] = jnp.zeros_like(acc_sc)

    g = g_ref[...]
    o = o_ref[0] * (g * jax.nn.sigmoid(g))
    y = jnp.dot(o.astype(BF16), w_ref[0], preferred_element_type=F32)
    acc_sc[...] += jax.nn.sigmoid(mg_ref[...] + b_ref[0]) * y

    @pl.when(n == N_BRANCH - 1)
    def _():
        z_ref[...] = acc_sc[...].astype(z_ref.dtype)


def pl_branch_merge(outs, c_qg, merge, b_merge, w_branch):
    _, m, bw = outs.shape
    d = w_branch.shape[2]
    tm = _tile(m, 544, 16)
    tn = _tile(d, 1024, 128)
    nj = d // tn
    return pl.pallas_call(
        _branch_merge_kernel,
        out_shape=jax.ShapeDtypeStruct((m, d), BF16),
        grid=(m // tm, nj, N_BRANCH),
        in_specs=[pl.BlockSpec((1, tm, bw), lambda i, j, n: (n, i, 0)),
                  pl.BlockSpec((tm, bw), lambda i, j, n: (i, 2 * n)),
                  pl.BlockSpec((tm, tn), lambda i, j, n: (i, n * nj + j)),
                  pl.BlockSpec((1, 1, tn), lambda i, j, n: (n, 0, j)),
                  pl.BlockSpec((1, bw, tn), lambda i, j, n: (n, 0, j))],
        out_specs=pl.BlockSpec((tm, tn), lambda i, j, n: (i, j)),
        scratch_shapes=[pltpu.VMEM((tm, tn), F32)],
        compiler_params=_cparams(("arbitrary",) * 3),
        name="branch_merge",
    )(outs, c_qg, merge, b_merge.reshape(N_BRANCH, 1, d).astype(F32), w_branch)


def _rope_tables(pos, r, theta):
    half = r // 2
    inv = theta ** (-jnp.arange(half, dtype=F32) / half)
    ang = pos.astype(F32)[:, None] * inv[None, :]
    return jnp.cos(ang), jnp.sin(ang)


def _rope(x, cos, sin):
    half = x.shape[-1] // 2
    x1, x2 = x[..., :half], x[..., half:]
    c, s = cos[:, None, :], sin[:, None, :]
    return jnp.concatenate([x1 * c - x2 * s, x2 * c + x1 * s], axis=-1)


def _partial_rope(x, cos, sin):
    return jnp.concatenate([_rope(x[..., :ROT_B], cos, sin), x[..., ROT_B:]], axis=-1)


def _head_rmsnorm(x, g, heads):
    t = x.shape[0]
    d = x.shape[1] // heads
    return pl_rmsnorm(x.reshape(t * heads, d), g).reshape(t, heads, d)


def _prep_in_weights(w_in_l):
    def cols(name):
        off, size = IN_OFF[name]
        return w_in_l[:, off:off + size]
    small = [cols(n) for n in SMALL_COLS]
    used = sum(IN_OFF[n][1] for n in SMALL_COLS)
    small.append(jnp.zeros((w_in_l.shape[0], SMALL_W - used), w_in_l.dtype))
    w_small = jnp.concatenate(small, axis=1).astype(BF16)
    w_qg = jnp.concatenate([cols(n) for n in QG_COLS], axis=1).astype(BF16)
    w_merge = cols('merge').astype(BF16)
    return w_small, w_qg, w_merge


def _pad_page(x):
    return jnp.pad(x, ((0, 0), (0, PAGE_SIZE - x.shape[1]), (0, 0)))


def kernel(x_prompt, x_sample, mem_prompt, cache_mla_ckv, cache_mla_krope, cache_moba_k, cache_moba_v, cache_fox_k, cache_fox_v, cache_fox_logf, cache_mem_k, cache_mem_v, page_table, g_norm, w_in, g_q_lora, w_uq, g_kv_lora, w_uk, w_uv, g_mla_qn, g_mla_kn, g_mla_qr, g_mla_kr, g_moba_q, g_moba_k, g_fox_q, g_fox_k, b_fox_f, g_mem_norm, w_mem_kv, g_mem_q, g_mem_k, b_merge, w_branch, w_out):
    nb_, t_, d_ = x_prompt.shape
    ns_, tn_, _ = x_sample.shape
    n_mem = mem_prompt.shape[1]
    depth = w_in.shape[0]
    npg = page_table.shape[1]
    past_len = npg * PAGE_SIZE
    assert d_ == D_MODEL and tn_ <= PAGE_SIZE and past_len % MOBA_BLOCK == 0 and tn_ <= MOBA_BLOCK
    np_ = nb_ * t_
    nsr = ns_ * tn_
    nt = np_ + nsr

    pos = jnp.concatenate([jnp.arange(t_), past_len + jnp.arange(tn_)])
    cos_a, sin_a = _rope_tables(pos, ROPE_A, THETA_A)
    cos_b, sin_b = _rope_tables(pos, ROT_B, THETA_P)

    def tok_tables(c):
        cp = jnp.broadcast_to(c[None, :t_], (nb_, t_, c.shape[1])).reshape(np_, -1)
        cs = jnp.broadcast_to(c[None, t_:], (ns_, tn_, c.shape[1])).reshape(nsr, -1)
        return jnp.concatenate([cp, cs], axis=0)

    cos_a, sin_a, cos_b, sin_b = (tok_tables(c) for c in (cos_a, sin_a, cos_b, sin_b))

    x = jnp.concatenate([x_prompt.reshape(np_, d_), x_sample.reshape(nsr, d_)], axis=0)
    mem2d = mem_prompt.reshape(nb_ * n_mem, d_)
    page_table = page_table.astype(jnp.int32)

    st_p = {k: [] for k in ('ckv', 'kr', 'kb', 'vb', 'kc', 'vc', 'logf', 'mk', 'mv')}
    st_s = {k: [] for k in ('ckv', 'kr', 'kb', 'vb', 'kc', 'vc', 'logf')}

    for l in range(depth):
        w_small, w_qg, w_merge = _prep_in_weights(w_in[l])
        h = pl_rmsnorm(x, g_norm[l], BF16)
        c_small = pl_matmul(h, w_small, tn_cap=768)
        c_qg = pl_matmul(h, w_qg)
        merge = pl_matmul(h, w_merge)

        def sm(name):
            off = 0
            for n in SMALL_COLS:
                if n == name:
                    return c_small[:, off:off + IN_OFF[n][1]]
                off += IN_OFF[n][1]
            raise KeyError(name)

        def qg(name):
            i = QG_COLS.index(name)
            return c_qg[:, i * BR_W:(i + 1) * BR_W]

        cq = pl_rmsnorm(sm('a_cq'), g_q_lora[l], BF16)
        q_a = pl_matmul(cq, w_uq[l].astype(BF16), tn_cap=768).reshape(nt, H_A, NOPE_A + ROPE_A)
        qn = pl_rmsnorm(q_a[..., :NOPE_A].reshape(nt * H_A, NOPE_A), g_mla_qn[l]).reshape(nt, H_A, NOPE_A)
        qr = pl_rmsnorm(q_a[..., NOPE_A:].reshape(nt * H_A, ROPE_A), g_mla_qr[l]).reshape(nt, H_A, ROPE_A)
        qr = _rope(qr, cos_a, sin_a)
        ckv = pl_rmsnorm(sm('a_ckv'), g_kv_lora[l])
        kr = _rope(pl_rmsnorm(sm('a_kr'), g_mla_kr[l])[:, None, :], cos_a, sin_a)[:, 0, :]
        qb = _partial_rope(_head_rmsnorm(qg('b_q'), g_moba_q[l], H_B), cos_b, sin_b)
        kb = _partial_rope(pl_rmsnorm(sm('b_k'), g_moba_k[l])[:, None, :], cos_b, sin_b)[:, 0, :]
        vb = sm('b_v')
        qc = _head_rmsnorm(qg('c_q'), g_fox_q[l], H_C)
        kc = pl_rmsnorm(sm('c_k'), g_fox_k[l])
        vc = sm('c_v')
        logf = pl_log_sigmoid_bias(sm('c_f'), b_fox_f[l])
        qm = _head_rmsnorm(qg('m_q'), g_mem_q[l], H_M)

        hm = pl_rmsnorm(mem2d, g_mem_norm[l], BF16)
        kv = pl_matmul(hm, w_mem_kv[l].astype(BF16))
        mk = _head_rmsnorm(kv[:, :H_M * HD_M], g_mem_k[l], H_M).reshape(nb_, n_mem, H_M * HD_M)
        mv = kv[:, H_M * HD_M:].reshape(nb_, n_mem, H_M * HD_M)

        def prm(a):
            return a[:np_].reshape((nb_, t_) + a.shape[1:])

        def smp(a):
            return a[np_:].reshape((ns_, tn_) + a.shape[1:])

        def bhtd(a):
            return jnp.transpose(a, (0, 2, 1, 3))

        w_uk_b = w_uk[l].astype(BF16)
        w_uv_h = jnp.transpose(w_uv[l].reshape(KV_RANK, H_A, V_A), (1, 0, 2)).astype(BF16)

        kn_p = pl_rmsnorm(pl_matmul(ckv[:np_].astype(BF16), w_uk_b).reshape(np_ * H_A, NOPE_A), g_mla_kn[l])
        kn_p = kn_p.reshape(nb_, t_, H_A * NOPE_A)
        lat_p = pl_flash(bhtd(prm(qn)), kn_p, prm(ckv), scale=MLA_SCALE, causal=True,
                         q2=bhtd(prm(qr)), k2=prm(kr))
        o_a_p = pl_headproj(lat_p, w_uv_h).reshape(np_, BR_W)

        qb_p = bhtd(prm(qb))
        mask_p = pl_moba_gate_prompt(qb_p, prm(kb))
        o_b_p = pl_flash(qb_p, prm(kb), prm(vb), scale=MOBA_SCALE, causal=True, bmask=mask_p)
        o_b_p = jnp.transpose(o_b_p, (0, 2, 1, 3)).reshape(np_, BR_W)

        c_cum = pl_cumsum_rows(prm(logf))
        c_t = jnp.transpose(c_cum, (0, 2, 1))
        o_c_p = pl_flash(bhtd(prm(qc)), prm(kc), prm(vc), scale=FOX_SCALE, causal=True,
                         rowb=c_t[..., None], colb=-c_t[:, :, None, :])
        o_c_p = jnp.transpose(o_c_p, (0, 2, 1, 3)).reshape(np_, BR_W)

        o_m_p = pl_flash(bhtd(prm(qm)), mk, mv, scale=MEM_SCALE, causal=False, d_v=HD_M)
        o_m_p = jnp.transpose(o_m_p, (0, 2, 1, 3)).reshape(np_, BR_W)

        qn_s, qr_s = smp(qn), smp(qr)
        eye = jnp.eye(H_A, dtype=F32)
        qbd = (qn_s[:, :, :, None, :] * eye[None, None, :, :, None]).reshape(ns_, tn_ * H_A, H_A * NOPE_A)
        lat_s = pl_mla_decode(l, page_table, qbd, qr_s.reshape(ns_, tn_ * H_A, ROPE_A),
                              cache_mla_ckv, cache_mla_krope, _pad_page(smp(ckv)), _pad_page(smp(kr)),
                              w_uk_b, g_mla_kn[l])
        lat_s = jnp.transpose(lat_s.reshape(nsr, H_A, KV_RANK), (1, 0, 2))[None]
        o_a_s = pl_headproj(lat_s, w_uv_h).reshape(nsr, BR_W)

        qb_s = smp(qb).reshape(ns_, tn_ * H_B, HD_B)
        mask_s = pl_moba_gate_decode(l, page_table, qb_s, cache_moba_k)
        o_b_s = pl_moba_decode(l, page_table, qb_s, mask_s, cache_moba_k, cache_moba_v,
                               _pad_page(smp(kb)), _pad_page(smp(vb))).reshape(nsr, BR_W)

        o_c_s = pl_fox_decode(l, page_table, smp(qc).reshape(ns_, tn_ * H_C, HD_C),
                              cache_fox_k, cache_fox_v, cache_fox_logf,
                              _pad_page(smp(kc)), _pad_page(smp(vc)), _pad_page(smp(logf))).reshape(nsr, BR_W)

        qm_s = jnp.pad(bhtd(smp(qm)), ((0, 0), (0, 0), (0, 8 - tn_), (0, 0)))
        o_m_s = pl_flash(qm_s, cache_mem_k[l].reshape(ns_, n_mem, H_M * HD_M),
                         cache_mem_v[l].reshape(ns_, n_mem, H_M * HD_M), scale=MEM_SCALE, causal=False,
                         d_v=HD_M)
        o_m_s = jnp.transpose(o_m_s[:, :, :tn_], (0, 2, 1, 3)).reshape(nsr, BR_W)

        outs = jnp.stack([jnp.concatenate([o_a_p, o_a_s]), jnp.concatenate([o_b_p, o_b_s]),
                          jnp.concatenate([o_c_p, o_c_s]), jnp.concatenate([o_m_p, o_m_s])])
        z = pl_branch_merge(outs, c_qg, merge, b_merge[l], w_branch[l].astype(BF16))
        x = pl_matmul(z, w_out[l].astype(BF16), res=x, tn_cap=512)

        for k, a in (('ckv', ckv), ('kr', kr), ('kb', kb), ('vb', vb), ('kc', kc), ('vc', vc), ('logf', logf)):
            st_p[k].append(prm(a))
            st_s[k].append(smp(a))
        st_p['mk'].append(mk.reshape(nb_, n_mem, H_M, HD_M))
        st_p['mv'].append(mv.reshape(nb_, n_mem, H_M, HD_M))

    names = ('ckv', 'kr', 'kb', 'vb', 'kc', 'vc', 'logf')
    return ((x[:np_].reshape(nb_, t_, d_), x[np_:].reshape(ns_, tn_, d_))
            + tuple(jnp.stack(st_p[k]) for k in names)
            + (jnp.stack(st_p['mk']), jnp.stack(st_p['mv']))
            + tuple(jnp.stack(st_s[k]) for k in names))
```

```python
import functools

import jax
import jax.numpy as jnp
from jax import lax
from jax.experimental import pallas as pl
from jax.experimental.pallas import tpu as pltpu

F32 = jnp.float32
BF16 = jnp.bfloat16

D_MODEL = 4096
PAGE_SIZE = 128
N_BRANCH = 4
BR_W = D_MODEL // 4
EPS = 1e-6
NEG = -1e30
H_A, NOPE_A, ROPE_A, V_A, Q_RANK, KV_RANK = 8, 128, 64, 128, 768, 224
THETA_A = 10000.0
MLA_SCALE = (NOPE_A + ROPE_A) ** -0.5
H_B, HD_B = 16, 64
ROT_B = HD_B // 4
THETA_P = 500000.0
MOBA_BLOCK = 256
MOBA_TOPK = 3
MOBA_SCALE = HD_B ** -0.5
H_C, HD_C = 16, 64
FOX_SCALE = HD_C ** -0.5
H_M, HD_M = 4, 256
MEM_SCALE = HD_M ** -0.5

IN_SPLITS = (
    ('a_cq', Q_RANK), ('a_ckv', KV_RANK), ('a_kr', ROPE_A), ('a_gate', BR_W),
    ('b_q', H_B * HD_B), ('b_k', HD_B), ('b_v', HD_B), ('b_gate', BR_W),
    ('c_q', H_C * HD_C), ('c_k', HD_C), ('c_v', HD_C), ('c_f', H_C), ('c_gate', BR_W),
    ('m_q', H_M * HD_M), ('m_gate', BR_W),
    ('merge', N_BRANCH * D_MODEL),
)
IN_OFF = {}
_off = 0
for _name, _size in IN_SPLITS:
    IN_OFF[_name] = (_off, _size)
    _off += _size

SMALL_COLS = ('a_cq', 'a_ckv', 'a_kr', 'b_k', 'b_v', 'c_k', 'c_v', 'c_f')
SMALL_W = 1536
QG_COLS = ('a_gate', 'b_q', 'b_gate', 'c_q', 'c_gate', 'm_q', 'm_gate')
GATE_BLOCK = (0, 2, 4, 6)

VMEM_LIMIT = 56 * 1024 * 1024


def _tile(n, cap, mult=8):
    best = None
    for t in range(mult, min(n, cap) + 1, mult):
        if n % t == 0:
            best = t
    return best if best is not None else n


def _cparams(sem):
    return pltpu.CompilerParams(dimension_semantics=sem, vmem_limit_bytes=VMEM_LIMIT)


def _dot_nt(a, b):
    return lax.dot_general(a, b, (((1,), (1,)), ((), ())), preferred_element_type=F32)


def _split3(x):
    a = x.astype(BF16)
    r = x - a.astype(F32)
    b = r.astype(BF16)
    c = (r - b.astype(F32)).astype(BF16)
    return a, b, c


def _rmsnorm_kernel(x_ref, g_ref, o_ref):
    x = x_ref[...].astype(F32)
    r = lax.rsqrt(jnp.mean(x * x, axis=-1, keepdims=True) + EPS)
    o_ref[...] = ((x * r) * g_ref[...]).astype(o_ref.dtype)


def pl_rmsnorm(x, g, out_dtype=F32):
    m, d = x.shape
    tm = _tile(m, max(8, (2 * 1024 * 1024) // (4 * max(d, 128))), 8)
    return pl.pallas_call(
        _rmsnorm_kernel,
        out_shape=jax.ShapeDtypeStruct((m, d), out_dtype),
        grid=(m // tm,),
        in_specs=[pl.BlockSpec((tm, d), lambda i: (i, 0)),
                  pl.BlockSpec((1, d), lambda i: (0, 0))],
        out_specs=pl.BlockSpec((tm, d), lambda i: (i, 0)),
        compiler_params=_cparams(("arbitrary",)),
        name="rmsnorm",
    )(x, g.reshape(1, d).astype(F32))


def _logsig_kernel(x_ref, b_ref, o_ref):
    x = x_ref[...] + b_ref[...]
    o_ref[...] = jnp.minimum(x, 0.0) - jnp.log(1.0 + jnp.exp(-jnp.abs(x)))


def pl_log_sigmoid_bias(x, b):
    m, d = x.shape
    tm = _tile(m, 2048, 8)
    return pl.pallas_call(
        _logsig_kernel,
        out_shape=jax.ShapeDtypeStruct((m, d), F32),
        grid=(m // tm,),
        in_specs=[pl.BlockSpec((tm, d), lambda i: (i, 0)),
                  pl.BlockSpec((1, d), lambda i: (0, 0))],
        out_specs=pl.BlockSpec((tm, d), lambda i: (i, 0)),
        compiler_params=_cparams(("arbitrary",)),
        name="log_sigmoid_bias",
    )(x, b.reshape(1, d).astype(F32))


def _cumsum_kernel(x_ref, o_ref, carry_ref, *, blk):
    @pl.when(pl.program_id(1) == 0)
    def _():
        carry_ref[...] = jnp.zeros_like(carry_ref)

    row = lax.broadcasted_iota(jnp.int32, (blk, blk), 0)
    col = lax.broadcasted_iota(jnp.int32, (blk, blk), 1)
    tri = (col <= row).astype(BF16)
    x = x_ref[0]
    c = None
    for part in reversed(_split3(x)):
        t = jnp.dot(tri, part, preferred_element_type=F32)
        c = t if c is None else c + t
    c = c + carry_ref[...]
    o_ref[0] = c
    carry_ref[...] = c[blk - 1:blk, :]


def pl_cumsum_rows(x):
    b, t, w = x.shape
    blk = _tile(t, 256, 8)
    return pl.pallas_call(
        functools.partial(_cumsum_kernel, blk=blk),
        out_shape=jax.ShapeDtypeStruct((b, t, w), F32),
        grid=(b, t // blk),
        in_specs=[pl.BlockSpec((1, blk, w), lambda i, j: (i, j, 0))],
        out_specs=pl.BlockSpec((1, blk, w), lambda i, j: (i, j, 0)),
        scratch_shapes=[pltpu.VMEM((1, w), F32)],
        compiler_params=_cparams(("arbitrary", "arbitrary")),
        name="cumsum_rows",
    )(x)


def _mm_kernel(a_ref, w_ref, *rest, has_res):
    if has_res:
        r_ref, o_ref = rest
    else:
        (o_ref,) = rest
    acc = jnp.dot(a_ref[...], w_ref[...], preferred_element_type=F32)
    if has_res:
        acc = acc + r_ref[...]
    o_ref[...] = acc.astype(o_ref.dtype)


def pl_matmul(a, w, res=None, out_dtype=F32, tm_cap=1088, tn_cap=1024):
    m, k = a.shape
    _, n = w.shape
    tm = _tile(m, tm_cap, 16)
    tn = _tile(n, tn_cap, 128)
    in_specs = [pl.BlockSpec((tm, k), lambda i, j: (i, 0)),
                pl.BlockSpec((k, tn), lambda i, j: (0, j))]
    args = [a, w]
    if res is not None:
        in_specs.append(pl.BlockSpec((tm, tn), lambda i, j: (i, j)))
        args.append(res)
    return pl.pallas_call(
        functools.partial(_mm_kernel, has_res=res is not None),
        out_shape=jax.ShapeDtypeStruct((m, n), out_dtype),
        grid=(m // tm, n // tn),
        in_specs=in_specs,
        out_specs=pl.BlockSpec((tm, tn), lambda i, j: (i, j)),
        compiler_params=_cparams(("arbitrary", "arbitrary")),
        name="matmul",
    )(*args)


def _headproj_kernel(a_ref, w_ref, o_ref):
    o_ref[0] = jnp.dot(a_ref[0, 0].astype(BF16), w_ref[0], preferred_element_type=F32)


def pl_headproj(a, w):
    b, h, t, r = a.shape
    d = w.shape[2]
    tm = _tile(t, 1024, 8)
    return pl.pallas_call(
        _headproj_kernel,
        out_shape=jax.ShapeDtypeStruct((b, t, h * d), F32),
        grid=(b, h, t // tm),
        in_specs=[pl.BlockSpec((1, 1, tm, r), lambda bi, hi, i: (bi, hi, i, 0)),
                  pl.BlockSpec((1, r, d), lambda bi, hi, i: (hi, 0, 0))],
        out_specs=pl.BlockSpec((1, tm, d), lambda bi, hi, i: (bi, i, hi)),
        compiler_params=_cparams(("arbitrary", "arbitrary", "arbitrary")),
        name="headproj",
    )(a, w)


def _online_update(s, v_b, m_sc, l_sc, acc_sc):
    m_prev = m_sc[...]
    m_new = jnp.maximum(m_prev, jnp.max(s, axis=-1, keepdims=True))
    alpha = jnp.exp(m_prev - m_new)
    p = jnp.exp(s - m_new)
    l_sc[...] = alpha * l_sc[...] + jnp.sum(p, axis=-1, keepdims=True)
    acc_sc[...] = alpha * acc_sc[...] + jnp.dot(p.astype(BF16), v_b, preferred_element_type=F32)
    m_sc[...] = m_new


def _flash_kernel(*refs, scale, causal, tq, tk, nk, hrows, two, bias, bmask):
    it = iter(refs)
    q1_ref, k1_ref = next(it), next(it)
    q2_ref = k2_ref = rb_ref = cb_ref = mk_ref = None
    if two:
        q2_ref, k2_ref = next(it), next(it)
    v_ref = next(it)
    if bias:
        rb_ref, cb_ref = next(it), next(it)
    if bmask:
        mk_ref = next(it)
    o_ref, m_sc, l_sc, acc_sc = next(it), next(it), next(it), next(it)
    i = pl.program_id(2)
    j = pl.program_id(3)

    @pl.when(j == 0)
    def _():
        m_sc[...] = jnp.full_like(m_sc, NEG)
        l_sc[...] = jnp.zeros_like(l_sc)
        acc_sc[...] = jnp.zeros_like(acc_sc)

    def step():
        s = _dot_nt(q1_ref[0, 0].astype(BF16), k1_ref[0].astype(BF16))
        if two:
            s = s + _dot_nt(q2_ref[0, 0].astype(BF16), k2_ref[0].astype(BF16))
        s = s * scale
        if bias:
            cb = cb_ref[0, 0]
            if hrows > 1:
                cb = jnp.broadcast_to(cb[None], (tq // hrows, hrows, tk)).reshape(tq, tk)
            s = s + rb_ref[0, 0] + cb
        if causal:
            qpos = i * tq + lax.broadcasted_iota(jnp.int32, (tq, tk), 0)
            if hrows > 1:
                qpos = qpos // hrows
            kpos = j * tk + lax.broadcasted_iota(jnp.int32, (tq, tk), 1)
            ok = kpos <= qpos
            if bmask:
                mk = mk_ref[0, 0]
                lane = lax.broadcasted_iota(jnp.int32, mk.shape, 1)
                sel = jnp.sum(jnp.where(lane == j, mk, 0.0), axis=-1, keepdims=True)
                qcol = i * tq + lax.broadcasted_iota(jnp.int32, (tq, 1), 0)
                if hrows > 1:
                    qcol = qcol // hrows
                sel = sel + jnp.where(qcol // MOBA_BLOCK == j, 1.0, 0.0)
                ok = ok & (sel > 0.5)
            s = jnp.where(ok, s, NEG)
        _online_update(s, v_ref[0].astype(BF16), m_sc, l_sc, acc_sc)

    if causal:
        pl.when(j * tk <= ((i + 1) * tq - 1) // hrows)(step)
    else:
        step()

    @pl.when(j == nk - 1)
    def _():
        o_ref[0, 0] = acc_sc[...] / l_sc[...]


def pl_flash(q1, k1, v, *, scale, causal, q2=None, k2=None, rowb=None, colb=None, bmask=None, d_v=None, blk_q=256, blk_k=256, hrows=1):
    b, h, tq_all, d1 = q1.shape
    tk_all = k1.shape[1]
    tq = _tile(tq_all, blk_q, 8 * hrows)
    tk = _tile(tk_all, blk_k, 8)
    nq, nk = tq_all // tq, tk_all // tk
    if causal:
        assert tq_all == tk_all * hrows
    if bmask is not None:
        assert causal and tk == MOBA_BLOCK

    def jclamp(i, j):
        return jnp.minimum(j, (((i + 1) * tq - 1) // hrows) // tk) if causal else j

    def kmap(arr, d):
        hk = arr.shape[2] // d
        return (lambda bi, hi, i, j: (bi, jclamp(i, j), hi if hk > 1 else 0))

    qmap = lambda bi, hi, i, j: (bi, hi, i, 0)
    in_specs = [pl.BlockSpec((1, 1, tq, d1), qmap), pl.BlockSpec((1, tk, d1), kmap(k1, d1))]
    args = [q1, k1]
    if q2 is not None:
        d2 = q2.shape[3]
        in_specs += [pl.BlockSpec((1, 1, tq, d2), qmap), pl.BlockSpec((1, tk, d2), kmap(k2, d2))]
        args += [q2, k2]
    dv = v.shape[2] if d_v is None else d_v
    in_specs.append(pl.BlockSpec((1, tk, dv), kmap(v, dv)))
    args.append(v)
    if rowb is not None:
        in_specs += [pl.BlockSpec((1, 1, tq, 1), qmap),
                     pl.BlockSpec((1, 1, hrows, tk), lambda bi, hi, i, j: (bi, hi, 0, jclamp(i, j)))]
        args += [rowb, colb]
    if bmask is not None:
        nb = bmask.shape[3]
        in_specs.append(pl.BlockSpec((1, 1, tq, nb), qmap))
        args.append(bmask)
    kern = functools.partial(_flash_kernel, scale=scale, causal=causal, tq=tq, tk=tk, nk=nk, hrows=hrows,
                             two=q2 is not None, bias=rowb is not None, bmask=bmask is not None)
    return pl.pallas_call(
        kern,
        out_shape=jax.ShapeDtypeStruct((b, h, tq_all, dv), F32),
        grid=(b, h, nq, nk),
        in_specs=in_specs,
        out_specs=pl.BlockSpec((1, 1, tq, dv), qmap),
        scratch_shapes=[pltpu.VMEM((tq, 1), F32), pltpu.VMEM((tq, 1), F32), pltpu.VMEM((tq, dv), F32)],
        compiler_params=_cparams(("arbitrary",) * 4),
        name="flash_attention",
    )(*args)


def _topk_mask(g, valid, k):
    r, nb = g.shape
    g = jnp.where(valid, g, NEG)
    lane = lax.broadcasted_iota(jnp.int32, (r, nb), 1)
    rank = jnp.zeros((r, nb), jnp.int32)
    for m in range(nb):
        gm = g[:, m:m + 1]
        ahead = (gm > g) | ((gm == g) & (m < lane))
        rank = rank + ahead.astype(jnp.int32)
    return jnp.where((rank < k) & valid, 1.0, 0.0).astype(F32)


def _moba_gate_prompt_kernel(q_ref, k_ref, o_ref, *, nb, tt):
    i = pl.program_id(1)
    means = []
    for n in range(nb):
        kb = k_ref[0, n * MOBA_BLOCK:(n + 1) * MOBA_BLOCK, :]
        means.append(jnp.sum(kb, axis=0, keepdims=True) * (1.0 / MOBA_BLOCK))
    kmean = jnp.concatenate(means, axis=0)
    g = lax.dot_general(q_ref[0], kmean, (((1,), (1,)), ((), ())),
                        precision=lax.Precision.HIGHEST, preferred_element_type=F32)
    own = (i * tt) // MOBA_BLOCK
    lane = lax.broadcasted_iota(jnp.int32, g.shape, 1)
    o_ref[0] = _topk_mask(g, lane < own, MOBA_TOPK)


def pl_moba_gate_prompt(q, k, heads):
    b, rq, d = q.shape
    t = k.shape[1]
    assert t % MOBA_BLOCK == 0 and rq == t * heads
    nb = t // MOBA_BLOCK
    tt = MOBA_BLOCK // 2
    return pl.pallas_call(
        functools.partial(_moba_gate_prompt_kernel, nb=nb, tt=tt),
        out_shape=jax.ShapeDtypeStruct((b, rq, nb), F32),
        grid=(b, t // tt),
        in_specs=[pl.BlockSpec((1, tt * heads, d), lambda bi, i: (bi, i, 0)),
                  pl.BlockSpec((1, t, d), lambda bi, i: (bi, 0, 0))],
        out_specs=pl.BlockSpec((1, tt * heads, nb), lambda bi, i: (bi, i, 0)),
        compiler_params=_cparams(("arbitrary",) * 2),
        name="moba_gate_prompt",
    )(q, k)


PAGES_PER_STEP_CAP = 16


def _pages_per_step(npg):
    return _tile(npg, PAGES_PER_STEP_CAP, MOBA_BLOCK // PAGE_SIZE)


def _dec_init(m_sc, l_sc, acc_sc):
    m_sc[...] = jnp.full_like(m_sc, NEG)
    l_sc[...] = jnp.zeros_like(l_sc)
    acc_sc[...] = jnp.zeros_like(acc_sc)


def _online_update_pages(s_parts, v_parts, m_sc, l_sc, acc_sc):
    m_prev = m_sc[...]
    m_new = m_prev
    for s in s_parts:
        m_new = jnp.maximum(m_new, jnp.max(s, axis=-1, keepdims=True))
    alpha = jnp.exp(m_prev - m_new)
    l = alpha * l_sc[...]
    acc = alpha * acc_sc[...]
    for s, v in zip(s_parts, v_parts):
        p = jnp.exp(s - m_new)
        l = l + jnp.sum(p, axis=-1, keepdims=True)
        acc = acc + _dot_nt(p.astype(BF16), v)
    l_sc[...] = l
    acc_sc[...] = acc
    m_sc[...] = m_new


def _new_token_mask(rows, heads):
    jrow = lax.broadcasted_iota(jnp.int32, (rows, PAGE_SIZE), 0) // heads
    lane = lax.broadcasted_iota(jnp.int32, (rows, PAGE_SIZE), 1)
    return lane <= jrow, lane, jrow


def _paged_specs(layer, npg, g_pages, widths, reverse=False):
    nsteps = npg // g_pages
    specs = []
    for w in widths:
        for g in range(g_pages):
            def pmap(s, p, pt, g=g):
                if reverse:
                    idx = npg - jnp.clip(p, 1, nsteps) * g_pages + g
                else:
                    idx = jnp.minimum(p, nsteps - 1) * g_pages + g
                return (layer, pt[s, idx], 0, 0)
            specs.append(pl.BlockSpec((1, 1, w, PAGE_SIZE), pmap))
    return specs


def _mla_dec_kernel(pt_ref, qbd_ref, qr_ref, *refs, nsteps, g_pages):
    ckv_refs, kr_refs = refs[:g_pages], refs[g_pages:2 * g_pages]
    ckvn_ref, krn_ref, wukt_ref, g_ref, o_ref, m_sc, l_sc, acc_sc = refs[2 * g_pages:]
    p = pl.program_id(1)
    rows = qbd_ref.shape[1]

    @pl.when(p == 0)
    def _():
        _dec_init(m_sc, l_sc, acc_sc)

    def tile(pages, is_new):
        qg_b = (qbd_ref[0] * g_ref[...]).astype(BF16)
        qr_b = qr_ref[0].astype(BF16)
        if len(pages) % 2 == 0:
            pages = [(jnp.concatenate([pages[i][0], pages[i + 1][0]], axis=1),
                      jnp.concatenate([pages[i][1], pages[i + 1][1]], axis=1)) for i in range(0, len(pages), 2)]
        s_parts, v_parts = [], []
        for ckv, kr in pages:
            nkeys = ckv.shape[1]
            row_head = lax.broadcasted_iota(jnp.int32, (rows, nkeys), 0) % H_A
            ckv_b = ckv.astype(BF16)
            kn_t = jnp.dot(wukt_ref[...], ckv_b, preferred_element_type=F32)
            r_rows = jnp.zeros((rows, nkeys), F32)
            for h in range(H_A):
                blk = kn_t[h * NOPE_A:(h + 1) * NOPE_A, :]
                r = lax.rsqrt(jnp.mean(blk * blk, axis=0, keepdims=True) + EPS)
                r_rows = jnp.where(row_head == h, r, r_rows)
            s_nope = jnp.dot(qg_b, kn_t.astype(BF16), preferred_element_type=F32) * r_rows
            s = (s_nope + jnp.dot(qr_b, kr.astype(BF16), preferred_element_type=F32)) * MLA_SCALE
            if is_new:
                ok, _, _ = _new_token_mask(rows, H_A)
                s = jnp.where(ok, s, NEG)
            s_parts.append(s)
            v_parts.append(ckv_b)
        _online_update_pages(s_parts, v_parts, m_sc, l_sc, acc_sc)

    @pl.when(p < nsteps)
    def _():
        tile([(c[0, 0], k[0, 0]) for c, k in zip(ckv_refs, kr_refs)], False)

    @pl.when(p == nsteps)
    def _():
        tile([(ckvn_ref[0], krn_ref[0])], True)
        o_ref[0] = acc_sc[...] / l_sc[...]


def pl_mla_decode(layer, page_table, qbd, qr, cache_ckv, cache_kr, ckv_new, kr_new, w_uk_t, g_kn):
    s_, rows, _ = qbd.shape
    npg = page_table.shape[1]
    gp = _pages_per_step(npg)
    nsteps = npg // gp
    smap = lambda s, p, pt: (s, 0, 0)
    cmap = lambda s, p, pt: (0, 0)
    in_specs = ([pl.BlockSpec((1, rows, H_A * NOPE_A), smap), pl.BlockSpec((1, rows, ROPE_A), smap)]
                + _paged_specs(layer, npg, gp, (KV_RANK, ROPE_A))
                + [pl.BlockSpec((1, KV_RANK, PAGE_SIZE), smap), pl.BlockSpec((1, ROPE_A, PAGE_SIZE), smap),
                   pl.BlockSpec((H_A * NOPE_A, KV_RANK), cmap), pl.BlockSpec((1, H_A * NOPE_A), cmap)])
    return pl.pallas_call(
        functools.partial(_mla_dec_kernel, nsteps=nsteps, g_pages=gp),
        out_shape=jax.ShapeDtypeStruct((s_, rows, KV_RANK), F32),
        grid_spec=pltpu.PrefetchScalarGridSpec(
            num_scalar_prefetch=1, grid=(s_, nsteps + 1), in_specs=in_specs,
            out_specs=pl.BlockSpec((1, rows, KV_RANK), smap),
            scratch_shapes=[pltpu.VMEM((rows, 1), F32), pltpu.VMEM((rows, 1), F32),
                            pltpu.VMEM((rows, KV_RANK), F32)]),
        compiler_params=_cparams(("arbitrary", "arbitrary")),
        name="mla_decode",
    )(page_table, qbd, qr, *([cache_ckv] * gp), *([cache_kr] * gp), ckv_new, kr_new, w_uk_t,
      jnp.tile(g_kn.astype(F32), H_A).reshape(1, H_A * NOPE_A))


def _fox_dec_kernel(pt_ref, q_ref, *refs, nsteps, g_pages):
    k_refs, v_refs, lf_refs = (refs[:g_pages], refs[g_pages:2 * g_pages], refs[2 * g_pages:3 * g_pages])
    kn_ref, vn_ref, lfn_ref, o_ref, m_sc, l_sc, acc_sc, carry_sc = refs[3 * g_pages:]
    p = pl.program_id(1)
    rows = q_ref.shape[1]
    m_i = lax.broadcasted_iota(jnp.int32, (PAGE_SIZE, PAGE_SIZE), 0)
    k_i = lax.broadcasted_iota(jnp.int32, (PAGE_SIZE, PAGE_SIZE), 1)
    later = (m_i > k_i).astype(BF16)

    def page_sums(lf_t):
        x = jnp.concatenate([lf_t] * (rows // H_C), axis=0)
        sfx = None
        for part in reversed(_split3(x)):
            t = jnp.dot(part, later, preferred_element_type=F32)
            sfx = t if sfx is None else sfx + t
        return sfx, jnp.sum(x, axis=-1, keepdims=True)

    @pl.when(p == 0)
    def _():
        _dec_init(m_sc, l_sc, acc_sc)
        sfx, tot = page_sums(lfn_ref[0])
        ok, lane, jrow = _new_token_mask(rows, H_C)
        sq = jnp.sum(jnp.where(lane == jrow, sfx, 0.0), axis=-1, keepdims=True)
        s = jnp.dot(q_ref[0].astype(BF16), kn_ref[0].astype(BF16), preferred_element_type=F32) * FOX_SCALE
        s = jnp.where(ok, s + (sfx - sq), NEG)
        carry_sc[...] = tot - sq
        _online_update_pages([s], [vn_ref[0].astype(BF16)], m_sc, l_sc, acc_sc)

    @pl.when(p > 0)
    def _():
        q_b = q_ref[0].astype(BF16)
        sums = [page_sums(lf[0, 0]) for lf in lf_refs]
        after = carry_sc[...]
        s_parts = [None] * g_pages
        for g in reversed(range(g_pages)):
            sfx, tot = sums[g]
            s = jnp.dot(q_b, k_refs[g][0, 0].astype(BF16), preferred_element_type=F32) * FOX_SCALE
            s_parts[g] = s + (sfx + after)
            after = after + tot
        carry_sc[...] = after
        _online_update_pages(s_parts, [v[0, 0].astype(BF16) for v in v_refs], m_sc, l_sc, acc_sc)

    @pl.when(p == nsteps)
    def _():
        o_ref[0] = acc_sc[...] / l_sc[...]


def pl_fox_decode(layer, page_table, q, cache_k, cache_v, cache_lf, k_new, v_new, lf_new):
    s_, rows, d = q.shape
    npg = page_table.shape[1]
    gp = _pages_per_step(npg)
    nsteps = npg // gp
    smap = lambda s, p, pt: (s, 0, 0)
    in_specs = ([pl.BlockSpec((1, rows, d), smap)]
                + _paged_specs(layer, npg, gp, (HD_C, HD_C, H_C), reverse=True)
                + [pl.BlockSpec((1, HD_C, PAGE_SIZE), smap), pl.BlockSpec((1, HD_C, PAGE_SIZE), smap),
                   pl.BlockSpec((1, H_C, PAGE_SIZE), smap)])
    return pl.pallas_call(
        functools.partial(_fox_dec_kernel, nsteps=nsteps, g_pages=gp),
        out_shape=jax.ShapeDtypeStruct((s_, rows, HD_C), F32),
        grid_spec=pltpu.PrefetchScalarGridSpec(
            num_scalar_prefetch=1, grid=(s_, nsteps + 1), in_specs=in_specs,
            out_specs=pl.BlockSpec((1, rows, HD_C), smap),
            scratch_shapes=[pltpu.VMEM((rows, 1), F32), pltpu.VMEM((rows, 1), F32),
                            pltpu.VMEM((rows, HD_C), F32), pltpu.VMEM((rows, 1), F32)]),
        compiler_params=_cparams(("arbitrary", "arbitrary")),
        name="fox_decode",
    )(page_table, q, *([cache_k] * gp), *([cache_v] * gp), *([cache_lf] * gp), k_new, v_new, lf_new)


def _moba_gate_dec_kernel(pt_ref, q_ref, *refs, nsteps, g_pages, ppb):
    k_refs = refs[:g_pages]
    o_ref, km_sc = refs[g_pages:]
    p = pl.program_id(1)
    nbs = g_pages // ppb

    @pl.when(p == 0)
    def _():
        km_sc[...] = jnp.zeros_like(km_sc)

    km = km_sc[...]
    lane = lax.broadcasted_iota(jnp.int32, km.shape, 1)
    for b in range(nbs):
        acc = None
        for g in range(b * ppb, (b + 1) * ppb):
            t = jnp.sum(k_refs[g][0, 0], axis=1, keepdims=True)
            acc = t if acc is None else acc + t
        km = jnp.where(lane == p * nbs + b, acc, km)
    km_sc[...] = km

    @pl.when(p == nsteps - 1)
    def _():
        kmean = km_sc[...] * (1.0 / MOBA_BLOCK)
        g = lax.dot_general(q_ref[0], kmean, (((1,), (0,)), ((), ())),
                            precision=lax.Precision.HIGHEST, preferred_element_type=F32)
        o_ref[0] = _topk_mask(g, jnp.full(g.shape, True), MOBA_TOPK)


def pl_moba_gate_decode(layer, page_table, q, cache_k):
    s_, rows, d = q.shape
    npg = page_table.shape[1]
    ppb = MOBA_BLOCK // PAGE_SIZE
    gp = _pages_per_step(npg)
    assert npg % ppb == 0 and gp % ppb == 0
    nb = npg // ppb
    nsteps = npg // gp
    smap = lambda s, p, pt: (s, 0, 0)
    return pl.pallas_call(
        functools.partial(_moba_gate_dec_kernel, nsteps=nsteps, g_pages=gp, ppb=ppb),
        out_shape=jax.ShapeDtypeStruct((s_, rows, nb), F32),
        grid_spec=pltpu.PrefetchScalarGridSpec(
            num_scalar_prefetch=1, grid=(s_, nsteps),
            in_specs=[pl.BlockSpec((1, rows, d), smap)] + _paged_specs(layer, npg, gp, (HD_B,)),
            out_specs=pl.BlockSpec((1, rows, nb), smap),
            scratch_shapes=[pltpu.VMEM((HD_B, nb), F32)]),
        compiler_params=_cparams(("arbitrary", "arbitrary")),
        name="moba_gate_decode",
    )(page_table, q, *([cache_k] * gp))


def _moba_dec_kernel(pt_ref, q_ref, mk_ref, *refs, nsteps, g_pages, ppb):
    k_refs, v_refs = refs[:g_pages], refs[g_pages:2 * g_pages]
    kn_ref, vn_ref, o_ref, m_sc, l_sc, acc_sc = refs[2 * g_pages:]
    p = pl.program_id(1)
    rows = q_ref.shape[1]

    @pl.when(p == 0)
    def _():
        _dec_init(m_sc, l_sc, acc_sc)

    @pl.when(p < nsteps)
    def _():
        q_b = q_ref[0].astype(BF16)
        mk = mk_ref[0]
        lane = lax.broadcasted_iota(jnp.int32, mk.shape, 1)
        s_parts = []
        for g in range(g_pages):
            blk = p * (g_pages // ppb) + g // ppb
            ok = jnp.sum(jnp.where(lane == blk, mk, 0.0), axis=-1, keepdims=True) > 0.5
            s = jnp.dot(q_b, k_refs[g][0, 0].astype(BF16), preferred_element_type=F32) * MOBA_SCALE
            s_parts.append(jnp.where(ok, s, NEG))
        _online_update_pages(s_parts, [v[0, 0].astype(BF16) for v in v_refs], m_sc, l_sc, acc_sc)

    @pl.when(p == nsteps)
    def _():
        ok, _, _ = _new_token_mask(rows, H_B)
        s = jnp.dot(q_ref[0].astype(BF16), kn_ref[0].astype(BF16), preferred_element_type=F32) * MOBA_SCALE
        _online_update_pages([jnp.where(ok, s, NEG)], [vn_ref[0].astype(BF16)], m_sc, l_sc, acc_sc)
        o_ref[0] = acc_sc[...] / l_sc[...]


def pl_moba_decode(layer, page_table, q, mask, cache_k, cache_v, k_new, v_new):
    s_, rows, d = q.shape
    npg = page_table.shape[1]
    ppb = MOBA_BLOCK // PAGE_SIZE
    gp = _pages_per_step(npg)
    nsteps = npg // gp
    nb = mask.shape[2]
    smap = lambda s, p, pt: (s, 0, 0)
    in_specs = ([pl.BlockSpec((1, rows, d), smap), pl.BlockSpec((1, rows, nb), smap)]
                + _paged_specs(layer, npg, gp, (HD_B, HD_B))
                + [pl.BlockSpec((1, HD_B, PAGE_SIZE), smap), pl.BlockSpec((1, HD_B, PAGE_SIZE), smap)])
    return pl.pallas_call(
        functools.partial(_moba_dec_kernel, nsteps=nsteps, g_pages=gp, ppb=ppb),
        out_shape=jax.ShapeDtypeStruct((s_, rows, HD_B), F32),
        grid_spec=pltpu.PrefetchScalarGridSpec(
            num_scalar_prefetch=1, grid=(s_, nsteps + 1), in_specs=in_specs,
            out_specs=pl.BlockSpec((1, rows, HD_B), smap),
            scratch_shapes=[pltpu.VMEM((rows, 1), F32), pltpu.VMEM((rows, 1), F32),
                            pltpu.VMEM((rows, HD_B), F32)]),
        compiler_params=_cparams(("arbitrary", "arbitrary")),
        name="moba_decode",
    )(page_table, q, mask, *([cache_k] * gp), *([cache_v] * gp), k_new, v_new)


def _branch_merge_kernel(o_ref, g_ref, mg_ref, b_ref, w_ref, z_ref, acc_sc):
    n = pl.program_id(2)

    @pl.when(n == 0)
    def _():
        acc_sc[...] = jnp.zeros_like(acc_sc)

    g = g_ref[...]
    o = o_ref[0] * (g * jax.nn.sigmoid(g))
    y = jnp.dot(o.astype(BF16), w_ref[0], preferred_element_type=F32)
    acc_sc[...] += jax.nn.sigmoid(mg_ref[...] + b_ref[0]) * y

    @pl.when(n == N_BRANCH - 1)
    def _():
        z_ref[...] = acc_sc[...].astype(z_ref.dtype)


def pl_branch_merge(outs, c_qg, merge, b_merge, w_branch):
    _, m, bw = outs.shape
    d = w_branch.shape[2]
    tm = _tile(m, 544, 16)
    tn = _tile(d, 1024, 128)
    nj = d // tn
    return pl.pallas_call(
        _branch_merge_kernel,
        out_shape=jax.ShapeDtypeStruct((m, d), BF16),
        grid=(m // tm, nj, N_BRANCH),
        in_specs=[pl.BlockSpec((1, tm, bw), lambda i, j, n: (n, i, 0)),
                  pl.BlockSpec((tm, bw), lambda i, j, n: (i, 2 * n)),
                  pl.BlockSpec((tm, tn), lambda i, j, n: (i, n * nj + j)),
                  pl.BlockSpec((1, 1, tn), lambda i, j, n: (n, 0, j)),
                  pl.BlockSpec((1, bw, tn), lambda i, j, n: (n, 0, j))],
        out_specs=pl.BlockSpec((tm, tn), lambda i, j, n: (i, j)),
        scratch_shapes=[pltpu.VMEM((tm, tn), F32)],
        compiler_params=_cparams(("arbitrary",) * 3),
        name="branch_merge",
    )(outs, c_qg, merge, b_merge.reshape(N_BRANCH, 1, d).astype(F32), w_branch)


def _rope_tables(pos, r, theta):
    half = r // 2
    inv = theta ** (-jnp.arange(half, dtype=F32) / half)
    ang = pos.astype(F32)[:, None] * inv[None, :]
    return jnp.cos(ang), jnp.sin(ang)


def _rope(x, cos, sin):
    half = x.shape[-1] // 2
    x1, x2 = x[..., :half], x[..., half:]
    c, s = cos[:, None, :], sin[:, None, :]
    return jnp.concatenate([x1 * c - x2 * s, x2 * c + x1 * s], axis=-1)


def _partial_rope(x, cos, sin):
    return jnp.concatenate([_rope(x[..., :ROT_B], cos, sin), x[..., ROT_B:]], axis=-1)


def _head_rmsnorm(x, g, heads):
    t = x.shape[0]
    d = x.shape[1] // heads
    return pl_rmsnorm(x.reshape(t * heads, d), g).reshape(t, heads, d)


def _prep_in_weights(w_in_l):
    def cols(name):
        off, size = IN_OFF[name]
        return w_in_l[:, off:off + size]
    small = [cols(n) for n in SMALL_COLS]
    used = sum(IN_OFF[n][1] for n in SMALL_COLS)
    small.append(jnp.zeros((w_in_l.shape[0], SMALL_W - used), w_in_l.dtype))
    w_small = jnp.concatenate(small, axis=1).astype(BF16)
    w_qg = jnp.concatenate([cols(n) for n in QG_COLS], axis=1).astype(BF16)
    w_merge = cols('merge').astype(BF16)
    return w_small, w_qg, w_merge


def _pad_page(x):
    return jnp.swapaxes(jnp.pad(x, ((0, 0), (0, PAGE_SIZE - x.shape[1]), (0, 0))), 1, 2)


def _width_major(cache):
    return jnp.swapaxes(cache, 2, 3)


def kernel(x_prompt, x_sample, mem_prompt, cache_mla_ckv, cache_mla_krope, cache_moba_k, cache_moba_v, cache_fox_k, cache_fox_v, cache_fox_logf, cache_mem_k, cache_mem_v, page_table, g_norm, w_in, g_q_lora, w_uq, g_kv_lora, w_uk, w_uv, g_mla_qn, g_mla_kn, g_mla_qr, g_mla_kr, g_moba_q, g_moba_k, g_fox_q, g_fox_k, b_fox_f, g_mem_norm, w_mem_kv, g_mem_q, g_mem_k, b_merge, w_branch, w_out):
    nb_, t_, d_ = x_prompt.shape
    ns_, tn_, _ = x_sample.shape
    n_mem = mem_prompt.shape[1]
    depth = w_in.shape[0]
    npg = page_table.shape[1]
    past_len = npg * PAGE_SIZE
    assert d_ == D_MODEL and tn_ <= PAGE_SIZE and past_len % MOBA_BLOCK == 0 and tn_ <= MOBA_BLOCK
    np_ = nb_ * t_
    nsr = ns_ * tn_
    nt = np_ + nsr

    pos = jnp.concatenate([jnp.arange(t_), past_len + jnp.arange(tn_)])
    cos_a, sin_a = _rope_tables(pos, ROPE_A, THETA_A)
    cos_b, sin_b = _rope_tables(pos, ROT_B, THETA_P)

    def tok_tables(c):
        cp = jnp.broadcast_to(c[None, :t_], (nb_, t_, c.shape[1])).reshape(np_, -1)
        cs = jnp.broadcast_to(c[None, t_:], (ns_, tn_, c.shape[1])).reshape(nsr, -1)
        return jnp.concatenate([cp, cs], axis=0)

    cos_a, sin_a, cos_b, sin_b = (tok_tables(c) for c in (cos_a, sin_a, cos_b, sin_b))

    x = jnp.concatenate([x_prompt.reshape(np_, d_), x_sample.reshape(nsr, d_)], axis=0)
    mem2d = mem_prompt.reshape(nb_ * n_mem, d_)
    page_table = page_table.astype(jnp.int32)
    cache_mla_ckv, cache_mla_krope, cache_moba_k, cache_moba_v, cache_fox_k, cache_fox_v, cache_fox_logf = (
        _width_major(c) for c in (cache_mla_ckv, cache_mla_krope, cache_moba_k, cache_moba_v,
                                  cache_fox_k, cache_fox_v, cache_fox_logf))

    st_p = {k: [] for k in ('ckv', 'kr', 'kb', 'vb', 'kc', 'vc', 'logf', 'mk', 'mv')}
    st_s = {k: [] for k in ('ckv', 'kr', 'kb', 'vb', 'kc', 'vc', 'logf')}

    for l in range(depth):
        w_small, w_qg, w_merge = _prep_in_weights(w_in[l])
        h = pl_rmsnorm(x, g_norm[l], BF16)
        c_small = pl_matmul(h, w_small, tn_cap=768)
        c_qg = pl_matmul(h, w_qg)
        merge = pl_matmul(h, w_merge)

        def sm(name):
            off = 0
            for n in SMALL_COLS:
                if n == name:
                    return c_small[:, off:off + IN_OFF[n][1]]
                off += IN_OFF[n][1]
            raise KeyError(name)

        def qg(name):
            i = QG_COLS.index(name)
            return c_qg[:, i * BR_W:(i + 1) * BR_W]

        cq = pl_rmsnorm(sm('a_cq'), g_q_lora[l], BF16)
        q_a = pl_matmul(cq, w_uq[l].astype(BF16), tn_cap=768).reshape(nt, H_A, NOPE_A + ROPE_A)
        qn = pl_rmsnorm(q_a[..., :NOPE_A].reshape(nt * H_A, NOPE_A), g_mla_qn[l]).reshape(nt, H_A, NOPE_A)
        qr = pl_rmsnorm(q_a[..., NOPE_A:].reshape(nt * H_A, ROPE_A), g_mla_qr[l]).reshape(nt, H_A, ROPE_A)
        qr = _rope(qr, cos_a, sin_a)
        ckv = pl_rmsnorm(sm('a_ckv'), g_kv_lora[l])
        kr = _rope(pl_rmsnorm(sm('a_kr'), g_mla_kr[l])[:, None, :], cos_a, sin_a)[:, 0, :]
        qb = _partial_rope(_head_rmsnorm(qg('b_q'), g_moba_q[l], H_B), cos_b, sin_b)
        kb = _partial_rope(pl_rmsnorm(sm('b_k'), g_moba_k[l])[:, None, :], cos_b, sin_b)[:, 0, :]
        vb = sm('b_v')
        qc = _head_rmsnorm(qg('c_q'), g_fox_q[l], H_C)
        kc = pl_rmsnorm(sm('c_k'), g_fox_k[l])
        vc = sm('c_v')
        logf = pl_log_sigmoid_bias(sm('c_f'), b_fox_f[l])
        qm = _head_rmsnorm(qg('m_q'), g_mem_q[l], H_M)

        hm = pl_rmsnorm(mem2d, g_mem_norm[l], BF16)
        kv = pl_matmul(hm, w_mem_kv[l].astype(BF16))
        mk = _head_rmsnorm(kv[:, :H_M * HD_M], g_mem_k[l], H_M).reshape(nb_, n_mem, H_M * HD_M)
        mv = kv[:, H_M * HD_M:].reshape(nb_, n_mem, H_M * HD_M)

        def prm(a):
            return a[:np_].reshape((nb_, t_) + a.shape[1:])

        def smp(a):
            return a[np_:].reshape((ns_, tn_) + a.shape[1:])

        def bhtd(a):
            return jnp.transpose(a, (0, 2, 1, 3))

        w_uk_b = w_uk[l].astype(BF16)
        w_uv_h = jnp.transpose(w_uv[l].reshape(KV_RANK, H_A, V_A), (1, 0, 2)).astype(BF16)

        kn_p = pl_rmsnorm(pl_matmul(ckv[:np_].astype(BF16), w_uk_b).reshape(np_ * H_A, NOPE_A), g_mla_kn[l])
        kn_p = kn_p.reshape(nb_, t_, H_A * NOPE_A)
        lat_p = pl_flash(bhtd(prm(qn)), kn_p, prm(ckv), scale=MLA_SCALE, causal=True,
                         q2=bhtd(prm(qr)), k2=prm(kr), blk_q=512, blk_k=512)
        o_a_p = pl_headproj(lat_p, w_uv_h).reshape(np_, BR_W)

        qb_p = prm(qb).reshape(nb_, t_ * H_B, HD_B)
        mask_p = pl_moba_gate_prompt(qb_p, prm(kb), H_B)
        o_b_p = pl_flash(qb_p[:, None], prm(kb), prm(vb), scale=MOBA_SCALE, causal=True, bmask=mask_p[:, None],
                         blk_q=(MOBA_BLOCK // 2) * H_B, blk_k=MOBA_BLOCK, hrows=H_B).reshape(np_, BR_W)

        c_cum = pl_cumsum_rows(prm(logf))
        c_t = jnp.transpose(c_cum, (0, 2, 1))
        o_c_p = pl_flash(prm(qc).reshape(nb_, 1, t_ * H_C, HD_C), prm(kc), prm(vc), scale=FOX_SCALE, causal=True,
                         rowb=c_cum.reshape(nb_, 1, t_ * H_C, 1), colb=-c_t[:, None],
                         blk_q=128 * H_C, blk_k=256, hrows=H_C).reshape(np_, BR_W)

        o_m_p = pl_flash(bhtd(prm(qm)), mk, mv, scale=MEM_SCALE, causal=False, d_v=HD_M, blk_q=1024)
        o_m_p = jnp.transpose(o_m_p, (0, 2, 1, 3)).reshape(np_, BR_W)

        qn_s, qr_s = smp(qn), smp(qr)
        eye = jnp.eye(H_A, dtype=F32)
        qbd = (qn_s[:, :, :, None, :] * eye[None, None, :, :, None]).reshape(ns_, tn_ * H_A, H_A * NOPE_A)
        lat_s = pl_mla_decode(l, page_table, qbd, qr_s.reshape(ns_, tn_ * H_A, ROPE_A),
                              cache_mla_ckv, cache_mla_krope, _pad_page(smp(ckv)), _pad_page(smp(kr)),
                              w_uk_b.T, g_mla_kn[l])
        lat_s = jnp.transpose(lat_s.reshape(nsr, H_A, KV_RANK), (1, 0, 2))[None]
        o_a_s = pl_headproj(lat_s, w_uv_h).reshape(nsr, BR_W)

        qb_s = smp(qb).reshape(ns_, tn_ * H_B, HD_B)
        mask_s = pl_moba_gate_decode(l, page_table, qb_s, cache_moba_k)
        o_b_s = pl_moba_decode(l, page_table, qb_s, mask_s, cache_moba_k, cache_moba_v,
                               _pad_page(smp(kb)), _pad_page(smp(vb))).reshape(nsr, BR_W)

        o_c_s = pl_fox_decode(l, page_table, smp(qc).reshape(ns_, tn_ * H_C, HD_C),
                              cache_fox_k, cache_fox_v, cache_fox_logf,
                              _pad_page(smp(kc)), _pad_page(smp(vc)), _pad_page(smp(logf))).reshape(nsr, BR_W)

        qm_s = jnp.pad(bhtd(smp(qm)), ((0, 0), (0, 0), (0, 8 - tn_), (0, 0)))
        o_m_s = pl_flash(qm_s, cache_mem_k[l].reshape(ns_, n_mem, H_M * HD_M),
                         cache_mem_v[l].reshape(ns_, n_mem, H_M * HD_M), scale=MEM_SCALE, causal=False,
                         d_v=HD_M)
        o_m_s = jnp.transpose(o_m_s[:, :, :tn_], (0, 2, 1, 3)).reshape(nsr, BR_W)

        outs = jnp.stack([jnp.concatenate([o_a_p, o_a_s]), jnp.concatenate([o_b_p, o_b_s]),
                          jnp.concatenate([o_c_p, o_c_s]), jnp.concatenate([o_m_p, o_m_s])])
        z = pl_branch_merge(outs, c_qg, merge, b_merge[l], w_branch[l].astype(BF16))
        x = pl_matmul(z, w_out[l].astype(BF16), res=x, tn_cap=512)

        for k, a in (('ckv', ckv), ('kr', kr), ('kb', kb), ('vb', vb), ('kc', kc), ('vc', vc), ('logf', logf)):
            st_p[k].append(prm(a))
            st_s[k].append(smp(a))
        st_p['mk'].append(mk.reshape(nb_, n_mem, H_M, HD_M))
        st_p['mv'].append(mv.reshape(nb_, n_mem, H_M, HD_M))

    names = ('ckv', 'kr', 'kb', 'vb', 'kc', 'vc', 'logf')
    return ((x[:np_].reshape(nb_, t_, d_), x[np_:].reshape(ns_, tn_, d_))
            + tuple(jnp.stack(st_p[k]) for k in names)
            + (jnp.stack(st_p['mk']), jnp.stack(st_p['mv']))
            + tuple(jnp.stack(st_s[k]) for k in names))
```
